```python
import jax
import jax.numpy as jnp
from jax import lax
import numpy as np

D_MODEL = 2048
BATCH = 4
SEQ = 4096
DEPTH = 4

HEAD_DIM = 64
N_MIXERS = 4
GROUP_WIDTH = D_MODEL // N_MIXERS
GROUP_HEADS = GROUP_WIDTH // HEAD_DIM
BLOCK = 128
NORM_EPS = 1e-6
LN_EPS = 1e-5
NEG_INF = -1e30
RWKV_DECAY_RANK = 32
RWKV_ICLR_RANK = 32
RWKV_GATE_RANK = 96
RWKV_GN_EPS = 64e-5
RWKV_IN = 3 * GROUP_WIDTH + RWKV_DECAY_RANK + RWKV_ICLR_RANK + RWKV_GATE_RANK
SWA_KV_HEADS = 2
SWA_GROUPS = GROUP_HEADS // SWA_KV_HEADS
SWA_WINDOW = 128
SWA_IN = GROUP_WIDTH + 2 * SWA_KV_HEADS * HEAD_DIM
SGU_CHUNK = 128
SGU_TRI = SGU_CHUNK * (SGU_CHUNK + 1) // 2
SGU_IN = 2 * GROUP_WIDTH
FOX_IN = 4 * GROUP_WIDTH + GROUP_HEADS
N_IN = RWKV_IN + SWA_IN + SGU_IN + FOX_IN
FFN_HIDDEN = ((8 * D_MODEL + 3 * 256 - 1) // (3 * 256)) * 256

kernel_name = 'hybrid_rwkv7_swa_sgu_fox_adaln_trunk'


def _split(t, sizes):
    out, o = [], 0
    for s in sizes:
        out.append(t[..., o:o + s])
        o += s
    return out


def rms_norm(x, g, eps=NORM_EPS):
    xf = x.astype(jnp.float32)
    y = xf * lax.rsqrt(jnp.mean(xf * xf, axis=-1, keepdims=True) + eps)
    return y.astype(x.dtype) * g


def layer_norm(x, g, b, eps=LN_EPS):
    xf = x.astype(jnp.float32)
    mu = jnp.mean(xf, axis=-1, keepdims=True)
    var = jnp.mean(jnp.square(xf - mu), axis=-1, keepdims=True)
    return ((xf - mu) * lax.rsqrt(var + eps)).astype(x.dtype) * g + b


def wkv7_scan(r, w, k, v, a, b):
    bsz, _, nh, n = r.shape

    def step(S, inp):
        r_t, w_t, k_t, v_t, a_t, b_t = inp
        sa = jnp.einsum('bhij,bhj->bhi', S, a_t)
        S = S * w_t[:, :, None, :] + sa[..., None] * b_t[:, :, None, :] + v_t[..., None] * k_t[:, :, None, :]
        return S, jnp.einsum('bhij,bhj->bhi', S, r_t)

    xs = tuple(jnp.moveaxis(t, 1, 0) for t in (r, w, k, v, a, b))
    S0 = jnp.zeros((bsz, nh, n, n), jnp.float32)
    _, y = lax.scan(step, S0, xs)
    return jnp.moveaxis(y, 0, 1)


def rwkv7_mix(p, mu, w0, w2, a0, a2, g2, k_k, k_a, r_k, lnx_g, lnx_b):
    bsz, t, _ = p.shape
    p_prev = jnp.pad(p, ((0, 0), (1, 0), (0, 0)))[:, :-1]
    p = p + (p_prev - p) * mu
    r, k, v, wd, ad, gd = _split(p, (GROUP_WIDTH, GROUP_WIDTH, GROUP_WIDTH,
                                      RWKV_DECAY_RANK, RWKV_ICLR_RANK, RWKV_GATE_RANK))
    w = -jax.nn.softplus(-(w0 + jnp.tanh(wd) @ w2)) - 0.5
    a = jax.nn.sigmoid(a0 + ad @ a2)
    g = jax.nn.sigmoid(gd) @ g2
    hs = lambda z: z.reshape(bsz, t, GROUP_HEADS, HEAD_DIM).astype(jnp.float32)
    kk = hs(k * k_k)
    kk = kk / jnp.maximum(jnp.sqrt(jnp.sum(kk * kk, axis=-1, keepdims=True)), 1e-12)
    k = k * (1 + (a - 1) * k_a)
    rh, kh, vh, ah = hs(r), hs(k), hs(v), hs(a)
    decay = jnp.exp(-jnp.exp(hs(w)))
    y = wkv7_scan(rh, decay, kh, vh, -kk, kk * ah)
    mean = jnp.mean(y, axis=-1, keepdims=True)
    var = jnp.mean(jnp.square(y - mean), axis=-1, keepdims=True)
    y = (y - mean) * lax.rsqrt(var + RWKV_GN_EPS)
    y = y.reshape(bsz, t, GROUP_WIDTH).astype(p.dtype) * lnx_g + lnx_b
    bonus = jnp.sum(rh * kh * r_k.astype(jnp.float32), axis=-1, keepdims=True) * vh
    y = y + bonus.reshape(bsz, t, GROUP_WIDTH).astype(p.dtype)
    return y * g


def swa_sink_mix(p, qn_g, kn_g, sinks):
    bsz, t, _ = p.shape
    nb = t // BLOCK
    q, k, v = _split(p, (GROUP_WIDTH, SWA_KV_HEADS * HEAD_DIM, SWA_KV_HEADS * HEAD_DIM))
    q = rms_norm(q.reshape(bsz, t, GROUP_HEADS, HEAD_DIM), qn_g)
    k = rms_norm(k.reshape(bsz, t, SWA_KV_HEADS, HEAD_DIM), kn_g)
    q = q.reshape(bsz, nb, BLOCK, SWA_KV_HEADS, SWA_GROUPS, HEAD_DIM)
    k = k.reshape(bsz, nb, BLOCK, SWA_KV_HEADS, HEAD_DIM)
    v = v.reshape(bsz, nb, BLOCK, SWA_KV_HEADS, HEAD_DIM)
    shift = lambda z: jnp.pad(z, ((0, 0), (1, 0), (0, 0), (0, 0), (0, 0)))[:, :-1]
    k2 = jnp.concatenate([shift(k), k], axis=2)
    v2 = jnp.concatenate([shift(v), v], axis=2)
    s = jnp.einsum('bnqhgd,bnkhd->bnhgqk', q, k2).astype(jnp.float32) * (HEAD_DIM ** -0.5)
    i = jnp.arange(BLOCK)[:, None]
    j = jnp.arange(2 * BLOCK)[None, :]
    band = (j <= i + BLOCK) & (j > i + BLOCK - SWA_WINDOW)
    first = jnp.arange(nb)[:, None, None] > 0
    mask = band[None] & (first | (j >= BLOCK)[None])
    s = jnp.where(mask[None, :, None, None], s, NEG_INF)
    sink = sinks.astype(jnp.float32).reshape(SWA_KV_HEADS, SWA_GROUPS)[None, None, :, :, None, None]
    logits = jnp.concatenate([s, jnp.broadcast_to(sink, s.shape[:-1] + (1,))], axis=-1)
    probs = jax.nn.softmax(logits, axis=-1)[..., :-1]
    o = jnp.einsum('bnhgqk,bnkhd->bnqhgd', probs.astype(v2.dtype), v2)
    return o.reshape(bsz, t, GROUP_WIDTH)


def sgu_mix(p, ln_g, ln_b, ws_packed, bs):
    bsz, t, _ = p.shape
    nc = t // SGU_CHUNK
    u, v = _split(p, (GROUP_WIDTH, GROUP_WIDTH))
    u = jax.nn.gelu(u, approximate=False)
    v = layer_norm(jax.nn.gelu(v, approximate=False), ln_g, ln_b)
    rows, cols = np.tril_indices(SGU_CHUNK)
    w_s = jnp.zeros((GROUP_HEADS, SGU_CHUNK, SGU_CHUNK), ws_packed.dtype).at[:, rows, cols].set(ws_packed)
    v = v.reshape(bsz, nc, SGU_CHUNK, GROUP_HEADS, HEAD_DIM)
    z = jnp.einsum('gts,bnsgc->bntgc', w_s, v) + jnp.transpose(bs)[None, None, :, :, None]
    return u * z.reshape(bsz, t, GROUP_WIDTH)


def fox_mix(p, qn_g, kn_g, f_b):
    bsz, t, _ = p.shape
    nb = t // BLOCK
    q, k, v, fl, gl = _split(p, (GROUP_WIDTH, GROUP_WIDTH, GROUP_WIDTH, GROUP_HEADS, GROUP_WIDTH))
    q = rms_norm(q.reshape(bsz, t, GROUP_HEADS, HEAD_DIM), qn_g)
    k = rms_norm(k.reshape(bsz, t, GROUP_HEADS, HEAD_DIM), kn_g)
    v = v.reshape(bsz, t, GROUP_HEADS, HEAD_DIM)
    log_f = jax.nn.log_sigmoid((fl + f_b).astype(jnp.float32))
    cum = jnp.transpose(jnp.cumsum(log_f, axis=1), (0, 2, 1))
    qb = jnp.moveaxis(q.reshape(bsz, nb, BLOCK, GROUP_HEADS, HEAD_DIM), 1, 0)
    cb = jnp.moveaxis(cum.reshape(bsz, GROUP_HEADS, nb, BLOCK), 2, 0)
    s_pos = jnp.arange(t)

    def one_block(args):
        q_i, c_i, n = args
        s = jnp.einsum('bqhd,bkhd->bhqk', q_i, k).astype(jnp.float32) * (HEAD_DIM ** -0.5)
        s = s + c_i[..., None] - cum[:, :, None, :]
        t_pos = n * BLOCK + jnp.arange(BLOCK)
        s = jnp.where((s_pos[None, :] <= t_pos[:, None])[None, None], s, NEG_INF)
        pr = jax.nn.softmax(s, axis=-1)
        return jnp.einsum('bhqk,bkhd->bqhd', pr.astype(v.dtype), v)

    o = lax.map(one_block, (qb, cb, jnp.arange(nb)))
    o = jnp.moveaxis(o, 0, 1).reshape(bsz, t, GROUP_WIDTH)
    return o * jax.nn.sigmoid(gl)


def setup_inputs(seed: int = 0) -> dict:
    key = jax.random.key(seed)
    ks = iter(jax.random.split(key, 40))
    L, D, GW, H = DEPTH, D_MODEL, GROUP_WIDTH, GROUP_HEADS
    nrm = lambda shape, scale: jax.random.normal(next(ks), shape, jnp.float32) * scale
    gain = lambda shape: 1.0 + nrm(shape, 0.05)
    return {
        'x': nrm((BATCH, SEQ, D), 1.0),
        'c': nrm((BATCH, D), 1.0),
        'w_mod': nrm((L, D, 6 * D), 0.5 * D ** -0.5),
        'b_mod': nrm((L, 6 * D), 0.02),
        'norm1_g': gain((L, D)),
        'norm2_g': gain((L, D)),
        'w_in': nrm((L, D, N_IN), D ** -0.5),
        'w_out': nrm((L, D, D), D ** -0.5),
        'rwkv_mu': jax.random.uniform(next(ks), (L, RWKV_IN), jnp.float32),
        'rwkv_w0': jax.random.uniform(next(ks), (L, GW), jnp.float32, minval=-5.0, maxval=1.0),
        'rwkv_w2': nrm((L, RWKV_DECAY_RANK, GW), 0.5 * RWKV_DECAY_RANK ** -0.5),
        'rwkv_a0': nrm((L, GW), 0.5),
        'rwkv_a2': nrm((L, RWKV_ICLR_RANK, GW), 0.5 * RWKV_ICLR_RANK ** -0.5),
        'rwkv_g2': nrm((L, RWKV_GATE_RANK, GW), RWKV_GATE_RANK ** -0.5),
        'rwkv_kk': 0.85 + nrm((L, GW), 0.1),
        'rwkv_ka': 1.0 + nrm((L, GW), 0.1),
        'rwkv_rk': nrm((L, H, HEAD_DIM), 0.1),
        'rwkv_lnx_g': gain((L, GW)),
        'rwkv_lnx_b': nrm((L, GW), 0.02),
        'swa_qn_g': gain((L, HEAD_DIM)),
        'swa_kn_g': gain((L, HEAD_DIM)),
        'swa_sinks': nrm((L, H), 1.0),
        'sgu_ln_g': gain((L, GW)),
        'sgu_ln_b': nrm((L, GW), 0.02),
        'sgu_ws': nrm((L, H, SGU_TRI), SGU_CHUNK ** -0.5),
        'sgu_b': 1.0 + nrm((L, H, SGU_CHUNK), 0.1),
        'fox_qn_g': gain((L, HEAD_DIM)),
        'fox_kn_g': gain((L, HEAD_DIM)),
        'fox_fb': jax.random.uniform(next(ks), (L, H), jnp.float32, minval=1.0, maxval=3.0),
        'ffn_w1': nrm((L, D, FFN_HIDDEN), D ** -0.5),
        'ffn_w3': nrm((L, D, FFN_HIDDEN), D ** -0.5),
        'ffn_w2': nrm((L, FFN_HIDDEN, D), FFN_HIDDEN ** -0.5),
    }


def reference(x, c, w_mod, b_mod, norm1_g, norm2_g, w_in, w_out,
              rwkv_mu, rwkv_w0, rwkv_w2, rwkv_a0, rwkv_a2, rwkv_g2, rwkv_kk, rwkv_ka, rwkv_rk,
              rwkv_lnx_g, rwkv_lnx_b, swa_qn_g, swa_kn_g, swa_sinks,
              sgu_ln_g, sgu_ln_b, sgu_ws, sgu_b, fox_qn_g, fox_kn_g, fox_fb,
              ffn_w1, ffn_w3, ffn_w2):
    c_act = jax.nn.silu(c)
    for l in range(DEPTH):
        mod = c_act @ w_mod[l] + b_mod[l]
        shift1, scale1, gate1, shift2, scale2, gate2 = [m[:, None, :] for m in jnp.split(mod, 6, axis=-1)]
        h = rms_norm(x, norm1_g[l]) * (1 + scale1) + shift1
        proj = h @ w_in[l]
        pa, pb, pc, pd = _split(proj, (RWKV_IN, SWA_IN, SGU_IN, FOX_IN))
        ya = rwkv7_mix(pa, rwkv_mu[l], rwkv_w0[l], rwkv_w2[l], rwkv_a0[l], rwkv_a2[l], rwkv_g2[l],
                       rwkv_kk[l], rwkv_ka[l], rwkv_rk[l], rwkv_lnx_g[l], rwkv_lnx_b[l])
        yb = swa_sink_mix(pb, swa_qn_g[l], swa_kn_g[l], swa_sinks[l])
        yc = sgu_mix(pc, sgu_ln_g[l], sgu_ln_b[l], sgu_ws[l], sgu_b[l])
        yd = fox_mix(pd, fox_qn_g[l], fox_kn_g[l], fox_fb[l])
        y = jnp.concatenate([ya, yb, yc, yd], axis=-1) @ w_out[l]
        x = x + gate1 * y
        h = rms_norm(x, norm2_g[l]) * (1 + scale2) + shift2
        f = jax.nn.silu(h @ ffn_w1[l]) * (h @ ffn_w3[l])
        x = x + gate2 * (f @ ffn_w2[l])
    return x
```

```python
import functools

import numpy as np
import jax
import jax.numpy as jnp
from jax import lax
from jax.experimental import pallas as pl
from jax.experimental.pallas import tpu as pltpu

F32 = jnp.float32
BF16 = jnp.bfloat16

D_MODEL = 2048
HEAD_DIM = 64
GW = 512
GH = 8
BLOCK = 128
NORM_EPS = 1e-6
LN_EPS = 1e-5
NEG_INF = -1e30
RWKV_GN_EPS = 64e-5
RWKV_CHUNK = 64
FFN_HIDDEN = 5632

OFF_FQ, OFF_FK, OFF_FV, OFF_FG = 0, 512, 1024, 1536
OFF_SU, OFF_SV = 2048, 2560
OFF_RR = 3072
OFF_WQ = 4608
OFF_WKV = 5120
OFF_RLR = 5376
OFF_FF = 5632
NP = 5760

VMEM_LIMIT = 56 * 1024 * 1024


def _cparams(sem):
    return pltpu.CompilerParams(dimension_semantics=sem, vmem_limit_bytes=VMEM_LIMIT)


def _dot(a, b):
    return jnp.dot(a, b, preferred_element_type=F32)


def _dot_nt(a, b):
    return lax.dot_general(a, b, (((1,), (1,)), ((), ())), preferred_element_type=F32)


def _dot_tn(a, b):
    return lax.dot_general(a, b, (((0,), (0,)), ((), ())), preferred_element_type=F32)


def _seg_mean(x, e_ref):
    hi = x.astype(BF16)
    lo = (x - hi.astype(F32)).astype(BF16)
    e = e_ref[...]
    return _dot(hi, e) + _dot(lo, e)


def _half_masks(shape):
    lane = lax.broadcasted_iota(jnp.int32, shape, len(shape) - 1)
    m0 = (lane % 128) < 64
    return m0


def _mod_kernel(c_ref, w_ref, b_ref, o_ref):
    c = c_ref[...]
    ca = (c * jax.nn.sigmoid(c)).astype(BF16)
    o_ref[0] = _dot(ca, w_ref[0].astype(BF16)) + b_ref[0]


def _mod_call(c, w_mod, b_mod):
    L, D, N = w_mod.shape
    B = c.shape[0]
    tn = 1024
    return pl.pallas_call(
        _mod_kernel,
        grid=(L, N // tn),
        in_specs=[
            pl.BlockSpec((B, D), lambda l, j: (0, 0)),
            pl.BlockSpec((1, D, tn), lambda l, j: (l, 0, j)),
            pl.BlockSpec((1, 1, tn), lambda l, j: (l, 0, j)),
        ],
        out_specs=pl.BlockSpec((1, B, tn), lambda l, j: (l, 0, j)),
        out_shape=jax.ShapeDtypeStruct((L, B, N), F32),
        compiler_params=_cparams(("parallel", "parallel")),
        name="adaln_mod",
    )(c, w_mod, b_mod.reshape(L, 1, N))


def _in_kernel(x_ref, g_ref, sc_ref, sh_ref, w_ref, o_ref, h_ref):
    @pl.when(pl.program_id(1) == 0)
    def _():
        x = x_ref[...]
        ms = jnp.mean(x * x, axis=-1, keepdims=True)
        y = x * lax.rsqrt(ms + NORM_EPS) * g_ref[...]
        h_ref[...] = (y * (1.0 + sc_ref[0]) + sh_ref[0]).astype(BF16)

    o_ref[...] = _dot(h_ref[...], w_ref[...])


def _in_call(x2, g, scale, shift, w, T):
    M, D = x2.shape
    n = w.shape[1]
    tm, tn = 1024, 1152
    return pl.pallas_call(
        _in_kernel,
        grid=(M // tm, n // tn),
        in_specs=[
            pl.BlockSpec((tm, D), lambda i, j: (i, 0)),
            pl.BlockSpec((1, D), lambda i, j: (0, 0)),
            pl.BlockSpec((1, 1, D), lambda i, j: (i * tm // T, 0, 0)),
            pl.BlockSpec((1, 1, D), lambda i, j: (i * tm // T, 0, 0)),
            pl.BlockSpec((D, tn), lambda i, j: (0, j)),
        ],
        out_specs=pl.BlockSpec((tm, tn), lambda i, j: (i, j)),
        out_shape=jax.ShapeDtypeStruct((M, n), F32),
        scratch_shapes=[pltpu.VMEM((tm, D), BF16)],
        compiler_params=_cparams(("parallel", "arbitrary")),
        name="norm_in_proj",
    )(x2, g, scale, shift, w)


def _out_kernel(ya_ref, yb_ref, yc_ref, yd_ref, w_ref, x_ref, gt_ref, o_ref):
    acc = _dot(ya_ref[...], w_ref[0:GW, :])
    acc += _dot(yb_ref[...], w_ref[GW:2 * GW, :])
    acc += _dot(yc_ref[...], w_ref[2 * GW:3 * GW, :])
    acc += _dot(yd_ref[...], w_ref[3 * GW:4 * GW, :])
    o_ref[...] = x_ref[...] + gt_ref[0] * acc


def _out_call(ya, yb, yc, yd, w, x2, gate, T):
    M, D = x2.shape
    tm = 512
    yspec = pl.BlockSpec((tm, GW), lambda i: (i, 0))
    return pl.pallas_call(
        _out_kernel,
        grid=(M // tm,),
        in_specs=[
            yspec, yspec, yspec, yspec,
            pl.BlockSpec((D, D), lambda i: (0, 0)),
            pl.BlockSpec((tm, D), lambda i: (i, 0)),
            pl.BlockSpec((1, 1, D), lambda i: (i * tm // T, 0, 0)),
        ],
        out_specs=pl.BlockSpec((tm, D), lambda i: (i, 0)),
        out_shape=jax.ShapeDtypeStruct((M, D), F32),
        compiler_params=_cparams(("parallel",)),
        name="out_proj_residual",
    )(ya, yb, yc, yd, w, x2, gate)


def _ffn_kernel(x_ref, g_ref, sc_ref, sh_ref, gt_ref, w1_ref, w3_ref, w2_ref, o_ref, h_ref):
    @pl.when(pl.program_id(1) == 0)
    def _():
        x = x_ref[...]
        ms = jnp.mean(x * x, axis=-1, keepdims=True)
        y = x * lax.rsqrt(ms + NORM_EPS) * g_ref[...]
        h_ref[...] = (y * (1.0 + sc_ref[0]) + sh_ref[0]).astype(BF16)
        o_ref[...] = x

    h = h_ref[...]
    a = _dot(h, w1_ref[...])
    b = _dot(h, w3_ref[...])
    f = (a * jax.nn.sigmoid(a) * b).astype(BF16)
    o_ref[...] += gt_ref[0] * _dot(f, w2_ref[...])


def _ffn_call(x2, g, scale, shift, gate, w1, w3, w2, T):
    M, D = x2.shape
    F = w1.shape[1]
    tm, tf = 512, 512
    bspec = pl.BlockSpec((1, 1, D), lambda i, j: (i * tm // T, 0, 0))
    return pl.pallas_call(
        _ffn_kernel,
        grid=(M // tm, F // tf),
        in_specs=[
            pl.BlockSpec((tm, D), lambda i, j: (i, 0)),
            pl.BlockSpec((1, D), lambda i, j: (0, 0)),
            bspec, bspec, bspec,
            pl.BlockSpec((D, tf), lambda i, j: (0, j)),
            pl.BlockSpec((D, tf), lambda i, j: (0, j)),
            pl.BlockSpec((tf, D), lambda i, j: (j, 0)),
        ],
        out_specs=pl.BlockSpec((tm, D), lambda i, j: (i, 0)),
        out_shape=jax.ShapeDtypeStruct((M, D), F32),
        scratch_shapes=[pltpu.VMEM((tm, D), BF16)],
        compiler_params=_cparams(("parallel", "arbitrary")),
        name="swiglu_ffn",
    )(x2, g, scale, shift, gate, w1, w3, w2)


def _gelu(x):
    return 0.5 * x * (1.0 + lax.erf(x * np.float32(1.0 / np.sqrt(2.0))))


def _sgu_kernel(u_ref, v_ref, g_ref, b_ref, ws_ref, bias_ref, o_ref, *, chunks):
    m0 = _half_masks((BLOCK, 128))
    for c in range(chunks):
        rows = slice(c * BLOCK, (c + 1) * BLOCK)
        u = _gelu(u_ref[rows, :])
        v = _gelu(v_ref[rows, :])
        mu = jnp.mean(v, axis=-1, keepdims=True)
        vc = v - mu
        var = jnp.mean(vc * vc, axis=-1, keepdims=True)
        vn = vc * lax.rsqrt(var + LN_EPS) * g_ref[...] + b_ref[...]
        outs = []
        for j in range(GH // 2):
            vp = vn[:, 128 * j:128 * (j + 1)]
            stacked = jnp.concatenate(
                [jnp.where(m0, vp, 0.0), jnp.where(m0, 0.0, vp)], axis=0).astype(BF16)
            outs.append(_dot(ws_ref[j], stacked))
        z = jnp.concatenate(outs, axis=1) + bias_ref[...]
        o_ref[rows, :] = (u * z).astype(BF16)


def _sgu_call(proj, ln_g, ln_b, ws_cat, bias_full):
    M = proj.shape[0]
    chunks = 4
    tr = chunks * BLOCK
    return pl.pallas_call(
        functools.partial(_sgu_kernel, chunks=chunks),
        grid=(M // tr,),
        in_specs=[
            pl.BlockSpec((tr, GW), lambda i: (i, OFF_SU // GW)),
            pl.BlockSpec((tr, GW), lambda i: (i, OFF_SV // GW)),
            pl.BlockSpec((1, GW), lambda i: (0, 0)),
            pl.BlockSpec((1, GW), lambda i: (0, 0)),
            pl.BlockSpec((GH // 2, BLOCK, 2 * BLOCK), lambda i: (0, 0, 0)),
            pl.BlockSpec((BLOCK, GW), lambda i: (0, 0)),
        ],
        out_specs=pl.BlockSpec((tr, GW), lambda i: (i, 0)),
        out_shape=jax.ShapeDtypeStruct((M, GW), BF16),
        compiler_params=_cparams(("parallel",)),
        name="sgu_mix",
    )(proj, proj, ln_g, ln_b, ws_cat, bias_full)


def _swa_kernel(sink_ref, q_ref, kv_ref, kvp_ref, qg_ref, kg_ref, e_ref, o_ref, *, blocks, blocks_per_seq):
    i = pl.program_id(0)
    m0q = _half_masks((BLOCK, 128))
    m0k = _half_masks((2 * BLOCK, 128))
    row = lax.broadcasted_iota(jnp.int32, (BLOCK, 2 * BLOCK), 0)
    col = lax.broadcasted_iota(jnp.int32, (BLOCK, 2 * BLOCK), 1)
    band = (col <= row + BLOCK) & (col > row)
    e128 = e_ref.at[0:128, 0:128]

    kv_all = jnp.concatenate([kvp_ref[...], kv_ref[...]], axis=0)
    for r in range(blocks):
        first = ((i * blocks + r) % blocks_per_seq) == 0
        mask = band & (col >= jnp.where(first, BLOCK, 0))
        q = q_ref[r * BLOCK:(r + 1) * BLOCK, :]
        qn = q * lax.rsqrt(_seg_mean(q * q, e_ref) + NORM_EPS) * qg_ref[...]
        kv = kv_all[r * BLOCK:(r + 2) * BLOCK, :]
        k = kv[:, 0:128]
        v = kv[:, 128:256]
        kn = k * lax.rsqrt(_seg_mean(k * k, e128) + NORM_EPS) * kg_ref[...]
        kn_sw = pltpu.roll(kn, 64, axis=1)
        v_sw = pltpu.roll(v, 64, axis=1)
        kdup = [jnp.where(m0k, kn, kn_sw).astype(BF16), jnp.where(m0k, kn_sw, kn).astype(BF16)]
        vdup = [jnp.where(m0k, v, v_sw), jnp.where(m0k, v_sw, v)]
        vmsk = [[jnp.where(m0k, vd, 0.0).astype(BF16), jnp.where(m0k, 0.0, vd).astype(BF16)] for vd in vdup]
        outs = []
        for j in range(GH // 2):
            qp = qn[:, 128 * j:128 * (j + 1)]
            g = j // 2
            acc = None
            for half in range(2):
                h = 2 * j + half
                qm = (jnp.where(m0q, qp, 0.0) if half == 0 else jnp.where(m0q, 0.0, qp)).astype(BF16)
                s = _dot_nt(qm, kdup[g]) * np.float32(HEAD_DIM ** -0.5)
                s = jnp.where(mask, s, NEG_INF)
                sink = sink_ref[h]
                m = jnp.maximum(jnp.max(s, axis=-1, keepdims=True), sink)
                p = jnp.exp(s - m)
                den = jnp.sum(p, axis=-1, keepdims=True) + jnp.exp(sink - m)
                p = (p / den).astype(BF16)
                o = _dot(p, vmsk[g][half])
                acc = o if acc is None else acc + o
            outs.append(acc)
        o_ref[r * BLOCK:(r + 1) * BLOCK, :] = jnp.concatenate(outs, axis=1).astype(BF16)


def _swa_call(proj, sinks, qg, kg, e512, T):
    M = proj.shape[0]
    blocks = 2
    tr = blocks * BLOCK
    return pl.pallas_call(
        functools.partial(_swa_kernel, blocks=blocks, blocks_per_seq=T // BLOCK),
        grid=(M // tr,),
        in_specs=[
            pl.BlockSpec(memory_space=pltpu.SMEM),
            pl.BlockSpec((tr, GW), lambda i: (i, OFF_WQ // GW)),
            pl.BlockSpec((tr, 256), lambda i: (i, OFF_WKV // 256)),
            pl.BlockSpec((BLOCK, 256), lambda i: (jnp.maximum(i * blocks - 1, 0), OFF_WKV // 256)),
            pl.BlockSpec((1, GW), lambda i: (0, 0)),
            pl.BlockSpec((1, 128), lambda i: (0, 0)),
            pl.BlockSpec((GW, GW), lambda i: (0, 0)),
        ],
        out_specs=pl.BlockSpec((tr, GW), lambda i: (i, 0)),
        out_shape=jax.ShapeDtypeStruct((M, GW), BF16),
        compiler_params=_cparams(("parallel",)),
        name="swa_mix",
    )(sinks, proj, proj, proj, qg, kg, e512)


def _fox_prep_kernel(q_ref, k_ref, v_ref, f_ref, qg_ref, kg_ref, fb_ref, e_ref, tri_ref,
                     qo_ref, ko_ref, vo_ref, co_ref, carry_ref):
    @pl.when(pl.program_id(1) == 0)
    def _():
        carry_ref[...] = jnp.zeros_like(carry_ref)

    q = q_ref[...]
    k = k_ref[...]
    qn = q * lax.rsqrt(_seg_mean(q * q, e_ref) + NORM_EPS) * qg_ref[...]
    kn = k * lax.rsqrt(_seg_mean(k * k, e_ref) + NORM_EPS) * kg_ref[...]
    qo_ref[...] = (qn * np.float32(HEAD_DIM ** -0.5)).astype(BF16)
    ko_ref[...] = kn.astype(BF16)
    vo_ref[...] = v_ref[...].astype(BF16)
    z = f_ref[...] + fb_ref[...]
    logf = jnp.minimum(z, 0.0) - jnp.log1p(jnp.exp(-jnp.abs(z)))
    cum = jnp.dot(tri_ref[...], logf, preferred_element_type=F32,
                  precision=lax.Precision.HIGHEST) + carry_ref[...]
    co_ref[...] = cum
    carry_ref[...] = cum[cum.shape[0] - 1:cum.shape[0], :]


def _fox_prep_call(proj, qg, kg, fb, e512, T):
    M = proj.shape[0]
    tr = 256
    nb = T // tr
    tri = jnp.tril(jnp.ones((tr, tr), F32))
    cspec = lambda c: pl.BlockSpec((tr, GW), lambda b, i, c=c: (b * nb + i, c))
    ospec = pl.BlockSpec((tr, GW), lambda b, i: (b * nb + i, 0))
    return pl.pallas_call(
        _fox_prep_kernel,
        grid=(M // T, nb),
        in_specs=[
            cspec(OFF_FQ // GW), cspec(OFF_FK // GW), cspec(OFF_FV // GW),
            pl.BlockSpec((tr, 128), lambda b, i: (b * nb + i, OFF_FF // 128)),
            pl.BlockSpec((1, GW), lambda b, i: (0, 0)),
            pl.BlockSpec((1, GW), lambda b, i: (0, 0)),
            pl.BlockSpec((1, 128), lambda b, i: (0, 0)),
            pl.BlockSpec((GW, GW), lambda b, i: (0, 0)),
            pl.BlockSpec((tr, tr), lambda b, i: (0, 0)),
        ],
        out_specs=[ospec, ospec, ospec, pl.BlockSpec((tr, 128), lambda b, i: (b * nb + i, 0))],
        out_shape=[
            jax.ShapeDtypeStruct((M, GW), BF16),
            jax.ShapeDtypeStruct((M, GW), BF16),
            jax.ShapeDtypeStruct((M, GW), BF16),
            jax.ShapeDtypeStruct((M, 128), F32),
        ],
        scratch_shapes=[pltpu.VMEM((1, 128), F32)],
        compiler_params=_cparams(("parallel", "arbitrary")),
        name="fox_prep",
    )(proj, proj, proj, proj, qg, kg, fb, e512, tri)


def _fox_kernel(q_ref, k_ref, v_ref, cc_ref, cr_ref, gl_ref, o_ref, *, tq):
    j = pl.program_id(1)
    i = pl.program_id(2)
    m0 = _half_masks((tq, 128))
    q = q_ref[...]
    zero = jnp.zeros_like(q)
    qm = [jnp.where(m0, q, zero), jnp.where(m0, zero, q)]
    cc = cc_ref[...]
    lane = lax.broadcasted_iota(jnp.int32, (tq, 128), 1)
    ccol = []
    for half in range(2):
        h = 2 * j + half
        ccol.append(jnp.sum(jnp.where(lane == h, cc, 0.0), axis=-1, keepdims=True))

    def step(kb, carry, masked):
        ms, ls, accs = carry
        start = pl.multiple_of(kb * tq, tq)
        kblk = k_ref[pl.ds(start, tq), :]
        vblk = v_ref[pl.ds(start, tq), :]
        new_m, new_l, new_acc = [], [], []
        for half in range(2):
            crow = cr_ref[0, pl.ds(2 * j + half, 1), pl.ds(start, tq)]
            s = _dot_nt(qm[half], kblk)
            s = s + (ccol[half] - crow)
            if masked:
                r = lax.broadcasted_iota(jnp.int32, (tq, tq), 0)
                c = lax.broadcasted_iota(jnp.int32, (tq, tq), 1)
                s = jnp.where(c <= r, s, NEG_INF)
            m_new = jnp.maximum(ms[half], jnp.max(s, axis=-1, keepdims=True))
            alpha = jnp.exp(ms[half] - m_new)
            p = jnp.exp(s - m_new)
            new_l.append(alpha * ls[half] + jnp.sum(p, axis=-1, keepdims=True))
            new_acc.append(alpha * accs[half] + _dot(p.astype(BF16), vblk))
            new_m.append(m_new)
        return tuple(new_m), tuple(new_l), tuple(new_acc)

    init = (
        (jnp.full((tq, 1), NEG_INF, F32),) * 2,
        (jnp.zeros((tq, 1), F32),) * 2,
        (jnp.zeros((tq, 128), F32),) * 2,
    )
    carry = lax.fori_loop(0, i, lambda kb, c: step(kb, c, False), init)
    ms, ls, accs = step(i, carry, True)
    o = jnp.where(m0, accs[0] / ls[0], accs[1] / ls[1])
    o_ref[...] = (o * jax.nn.sigmoid(gl_ref[...])).astype(BF16)


def _fox_call(proj, qs, kn, vb, cum, cum_t, T):
    M = proj.shape[0]
    B = M // T
    tq = 256
    nq = T // tq
    return pl.pallas_call(
        functools.partial(_fox_kernel, tq=tq),
        grid=(B, GH // 2, nq),
        in_specs=[
            pl.BlockSpec((tq, 128), lambda b, j, i: (b * nq + i, j)),
            pl.BlockSpec((T, 128), lambda b, j, i: (b, j)),
            pl.BlockSpec((T, 128), lambda b, j, i: (b, j)),
            pl.BlockSpec((tq, 128), lambda b, j, i: (b * nq + i, 0)),
            pl.BlockSpec((1, GH, T), lambda b, j, i: (b, 0, 0)),
            pl.BlockSpec((tq, 128), lambda b, j, i: (b * nq + i, OFF_FG // 128 + j)),
        ],
        out_specs=pl.BlockSpec((tq, 128), lambda b, j, i: (b * nq + i, j)),
        out_shape=jax.ShapeDtypeStruct((M, GW), BF16),
        compiler_params=_cparams(("parallel", "parallel", "arbitrary")),
        name="fox_attn",
    )(qs, kn, vb, cum, cum_t, proj)


def _rwkv_local_kernel(p_ref, lr_ref, pp_ref, lrp_ref, mu_ref, mulr_ref, wlr_ref, w0_ref, a0_ref,
                       kk_ref, ka_ref, rk_ref, e_ref, tri_ref,
                       m_ref, n_ref, r_ref, y0_ref, bonus_ref, g_ref, *, chunks_per_seq):
    C = RWKV_CHUNK
    i = pl.program_id(0)
    has_prev = ((i % chunks_per_seq) != 0).astype(F32)
    row = lax.broadcasted_iota(jnp.int32, (C, 1), 0)

    def shifted(cur_ref, prev_ref, mu):
        cur = cur_ref[...]
        prev_row = prev_ref[7:8, :] * has_prev
        prev = jnp.where(row == 0, prev_row, pltpu.roll(cur, 1, axis=0))
        return cur + (prev - cur) * mu

    p = shifted(p_ref, pp_ref, mu_ref[...])
    lr = shifted(lr_ref, lrp_ref, mulr_ref[...])
    r = p[:, 0:GW]
    k = p[:, GW:2 * GW]
    v = p[:, 2 * GW:3 * GW]

    lane = lax.broadcasted_iota(jnp.int32, lr.shape, 1)
    z = jnp.where(lane < 32, jnp.tanh(lr), jnp.where(lane < 64, lr, jax.nn.sigmoid(lr)))
    low = _dot(z.astype(BF16), wlr_ref[...])
    x = -(w0_ref[...] + low[:, 0:GW])
    softplus = jnp.maximum(x, 0.0) + jnp.log1p(jnp.exp(-jnp.abs(x)))
    w = -softplus - 0.5
    lw = -jnp.exp(w)
    a = jax.nn.sigmoid(a0_ref[...] + low[:, GW:2 * GW])
    g_ref[...] = low[:, 2 * GW:3 * GW]

    kk = k * kk_ref[...]
    n2 = _seg_mean(kk * kk, e_ref) * np.float32(HEAD_DIM)
    kk = kk / jnp.maximum(jnp.sqrt(n2), 1e-12)
    k = k * (1.0 + (a - 1.0) * ka_ref[...])
    b = kk * a
    bonus_ref[...] = _seg_mean(r * k * rk_ref[...], e_ref) * np.float32(HEAD_DIM) * v

    L = jnp.dot(tri_ref[...], lw, preferred_element_type=F32, precision=lax.Precision.HIGHEST)
    Lend = L[C - 1:C, :]
    e_pos = jnp.exp(L)
    e_neg = jnp.exp(-L)
    e_hat = jnp.exp(Lend - L)
    at = -kk * jnp.exp(L - lw)
    rt = r * e_pos
    bt = b * e_neg
    kt = k * e_neg
    bh = b * e_hat
    kh = k * e_hat
    wc = jnp.exp(Lend)

    m0 = _half_masks((C, 128))
    ri = lax.broadcasted_iota(jnp.int32, (2 * C, 2 * C), 0)
    ci = lax.broadcasted_iota(jnp.int32, (2 * C, 2 * C), 1)
    strict = (ri % C) > (ci % C)
    incl = (ri % C) >= (ci % C)
    eye = ri == ci

    def stack(xp):
        return jnp.concatenate([jnp.where(m0, xp, 0.0), jnp.where(m0, 0.0, xp)], axis=0)

    for j in range(GH // 2):
        sl = slice(128 * j, 128 * (j + 1))
        ast, rst = stack(at[:, sl]), stack(rt[:, sl])
        bst, kst = stack(bt[:, sl]), stack(kt[:, sl])
        vst = stack(v[:, sl])
        bhst, khst = stack(bh[:, sl]), stack(kh[:, sl])
        ast_b = ast.astype(BF16)
        vst_b = vst.astype(BF16)
        left = jnp.concatenate([ast_b, rst.astype(BF16)], axis=0)
        right = jnp.concatenate([bst, kst], axis=0).astype(BF16)
        G = _dot_nt(left, right)
        a_ab = jnp.where(strict, G[0:2 * C, 0:2 * C], 0.0)
        a_ak = jnp.where(strict, G[0:2 * C, 2 * C:4 * C], 0.0)
        a_rb = jnp.where(incl, G[2 * C:4 * C, 0:2 * C], 0.0)
        a_rk = jnp.where(incl, G[2 * C:4 * C, 2 * C:4 * C], 0.0)
        t_inv = jnp.where(eye, 1.0, a_ab)
        apow = a_ab
        for _ in range(5):
            apow_b = apow.astype(BF16)
            apow = _dot(apow_b, apow_b)
            t_inv = t_inv + _dot(apow.astype(BF16), t_inv.astype(BF16))
        akv = _dot(a_ak.astype(BF16), vst_b)
        pq = _dot(t_inv.astype(BF16), jnp.concatenate([ast_b, akv.astype(BF16)], axis=1))
        rhs2 = jnp.concatenate(
            [pq.astype(BF16), jnp.concatenate([jnp.zeros_like(vst_b), vst_b], axis=1)], axis=0)
        ry = _dot(jnp.concatenate([a_rb, a_rk], axis=1).astype(BF16), rhs2)
        rhat = rst + ry[:, 0:128]
        y0 = ry[:, 128:256]
        r_ref[:, sl] = rhat[0:C, :] + rhat[C:2 * C, :]
        y0_ref[:, sl] = y0[0:C, :] + y0[C:2 * C, :]
        mn = _dot_tn(jnp.concatenate([bhst, khst], axis=0).astype(BF16), rhs2)
        m_ref[0, j] = mn[:, 0:128] + jnp.where(eye, wc[:, sl], 0.0)
        n_ref[0, j] = mn[:, 128:256]


def _rwkv_local_call(proj, mu_rkv, mu_lr, wlr, w0, a0, kk, ka, rk, e512, T):
    M = proj.shape[0]
    C = RWKV_CHUNK
    nchunks = M // C
    tri = jnp.tril(jnp.ones((C, C), F32))
    vec = lambda n: pl.BlockSpec((1, n), lambda i: (0, 0))
    rowspec = pl.BlockSpec((C, GW), lambda i: (i, 0))
    matspec = pl.BlockSpec((1, GH // 2, 128, 128), lambda i: (i, 0, 0, 0))
    prev = lambda i: jnp.maximum(i * (C // 8) - 1, 0)
    return pl.pallas_call(
        functools.partial(_rwkv_local_kernel, chunks_per_seq=T // C),
        grid=(nchunks,),
        in_specs=[
            pl.BlockSpec((C, 3 * GW), lambda i: (i, OFF_RR // (3 * GW))),
            pl.BlockSpec((C, 256), lambda i: (i, OFF_RLR // 256)),
            pl.BlockSpec((8, 3 * GW), lambda i: (prev(i), OFF_RR // (3 * GW))),
            pl.BlockSpec((8, 256), lambda i: (prev(i), OFF_RLR // 256)),
            vec(3 * GW), vec(256),
            pl.BlockSpec((256, 3 * GW), lambda i: (0, 0)),
            vec(GW), vec(GW), vec(GW), vec(GW), vec(GW),
            pl.BlockSpec((GW, GW), lambda i: (0, 0)),
            pl.BlockSpec((C, C), lambda i: (0, 0)),
        ],
        out_specs=[matspec, matspec, rowspec, rowspec, rowspec, rowspec],
        out_shape=[
            jax.ShapeDtypeStruct((nchunks, GH // 2, 128, 128), F32),
            jax.ShapeDtypeStruct((nchunks, GH // 2, 128, 128), F32),
            jax.ShapeDtypeStruct((M, GW), F32),
            jax.ShapeDtypeStruct((M, GW), F32),
            jax.ShapeDtypeStruct((M, GW), F32),
            jax.ShapeDtypeStruct((M, GW), F32),
        ],
        compiler_params=_cparams(("parallel",)),
        name="rwkv_chunk_local",
    )(proj, proj, proj, proj, mu_rkv, mu_lr, wlr, w0, a0, kk, ka, rk, e512, tri)


def _rwkv_state_kernel(m_ref, n_ref, r_ref, y0_ref, bonus_ref, g_ref, lg_ref, lb_ref, e_ref,
                       o_ref, h_ref, *, batch):
    @pl.when(pl.program_id(0) == 0)
    def _():
        h_ref[...] = jnp.zeros_like(h_ref)

    for b in range(batch):
        ys = []
        for j in range(GH // 2):
            sl = slice(128 * j, 128 * (j + 1))
            h = h_ref[b, j].astype(BF16)
            ys.append(_dot(r_ref[b, :, sl].astype(BF16), h) + y0_ref[b, :, sl])
            h_ref[b, j] = _dot(m_ref[b, 0, j].astype(BF16), h) + n_ref[b, 0, j]
        y = jnp.concatenate(ys, axis=1)
        mean = _seg_mean(y, e_ref)
        yc = y - mean
        var = _seg_mean(yc * yc, e_ref)
        yn = yc * lax.rsqrt(var + RWKV_GN_EPS) * lg_ref[...] + lb_ref[...]
        o_ref[b] = ((yn + bonus_ref[b]) * g_ref[b]).astype(BF16)


def _rwkv_state_call(mm, nn, rr, y0, bonus, g, lnx_g, lnx_b, e512, B, T):
    C = RWKV_CHUNK
    nc = T // C
    mm = mm.reshape(B, nc, GH // 2, 128, 128)
    nn = nn.reshape(B, nc, GH // 2, 128, 128)
    r3 = lambda a: a.reshape(B, T, GW)
    matspec = pl.BlockSpec((B, 1, GH // 2, 128, 128), lambda c: (0, c, 0, 0, 0))
    rowspec = pl.BlockSpec((B, C, GW), lambda c: (0, c, 0))
    vec = pl.BlockSpec((1, GW), lambda c: (0, 0))
    out = pl.pallas_call(
        functools.partial(_rwkv_state_kernel, batch=B),
        grid=(nc,),
        in_specs=[matspec, matspec, rowspec, rowspec, rowspec, rowspec, vec, vec,
                  pl.BlockSpec((GW, GW), lambda c: (0, 0))],
        out_specs=rowspec,
        out_shape=jax.ShapeDtypeStruct((B, T, GW), BF16),
        scratch_shapes=[pltpu.VMEM((B, GH // 2, 128, 128), F32)],
        compiler_params=_cparams(("arbitrary",)),
        name="rwkv_state_scan",
    )(mm, nn, r3(rr), r3(y0), r3(bonus), r3(g), lnx_g, lnx_b, e512)
    return out.reshape(B * T, GW)


def _pad_cols(a, n):
    return jnp.pad(a, [(0, 0)] * (a.ndim - 1) + [(0, n - a.shape[-1])])


def _permute_in_cols(w):
    rw, sw, sg, fx = 0, 1696, 2464, 3488
    parts = [
        w[..., fx:fx + 1536],
        w[..., fx + 1544:fx + 2056],
        w[..., sg:sg + 1024],
        w[..., rw:rw + 1536],
        w[..., sw:sw + 768],
        _pad_cols(w[..., rw + 1536:rw + 1696], 256),
        _pad_cols(w[..., fx + 1536:fx + 1544], 128),
    ]
    return jnp.concatenate(parts, axis=-1)


def _tile_heads(g):
    return jnp.tile(g, GH).reshape(1, GW)


def kernel(x, c, w_mod, b_mod, norm1_g, norm2_g, w_in, w_out, rwkv_mu, rwkv_w0, rwkv_w2, rwkv_a0, rwkv_a2, rwkv_g2, rwkv_kk, rwkv_ka, rwkv_rk, rwkv_lnx_g, rwkv_lnx_b, swa_qn_g, swa_kn_g, swa_sinks, sgu_ln_g, sgu_ln_b, sgu_ws, sgu_b, fox_qn_g, fox_kn_g, fox_fb, ffn_w1, ffn_w3, ffn_w2):
    B, T, D = x.shape
    L = w_mod.shape[0]
    M = B * T

    mod = _mod_call(c, w_mod, b_mod)
    mod = mod.reshape(L, B, 6, 1, D)

    seg = jnp.arange(GW) // HEAD_DIM
    e512 = jnp.where(seg[:, None] == seg[None, :], 1.0 / HEAD_DIM, 0.0).astype(BF16)
    rows, cols = np.tril_indices(BLOCK)

    x2 = x.reshape(M, D)
    for l in range(L):
        shift1, scale1, gate1, shift2, scale2, gate2 = [mod[l, :, s] for s in range(6)]
        w_in_l = _permute_in_cols(w_in[l]).astype(BF16)
        proj = _in_call(x2, norm1_g[l].reshape(1, D), scale1, shift1, w_in_l, T)

        mu = rwkv_mu[l]
        mu_rkv = mu[0:3 * GW].reshape(1, 3 * GW)
        mu_lr = _pad_cols(mu[3 * GW:], 256).reshape(1, 256)
        wlr = jnp.zeros((256, 3 * GW), F32)
        wlr = wlr.at[0:32, 0:GW].set(rwkv_w2[l])
        wlr = wlr.at[32:64, GW:2 * GW].set(rwkv_a2[l])
        wlr = wlr.at[64:160, 2 * GW:3 * GW].set(rwkv_g2[l])
        vec = lambda a: a.reshape(1, GW)
        mm, nn, rr, y0, bonus, g = _rwkv_local_call(
            proj, mu_rkv, mu_lr, wlr.astype(BF16), vec(rwkv_w0[l]), vec(rwkv_a0[l]), vec(rwkv_kk[l]),
            vec(rwkv_ka[l]), vec(rwkv_rk[l]), e512, T)
        ya = _rwkv_state_call(mm, nn, rr, y0, bonus, g, vec(rwkv_lnx_g[l]), vec(rwkv_lnx_b[l]), e512, B, T)

        yb = _swa_call(proj, swa_sinks[l], _tile_heads(swa_qn_g[l]), jnp.tile(swa_kn_g[l], 2).reshape(1, 128),
                       e512, T)

        w_s = jnp.zeros((GH, BLOCK, BLOCK), F32).at[:, rows, cols].set(sgu_ws[l])
        ws_cat = jnp.concatenate([w_s[0::2], w_s[1::2]], axis=2).astype(BF16)
        bias_full = jnp.repeat(jnp.transpose(sgu_b[l]), HEAD_DIM, axis=1)
        yc = _sgu_call(proj, vec(sgu_ln_g[l]), vec(sgu_ln_b[l]), ws_cat, bias_full)

        fb = _pad_cols(fox_fb[l], 128).reshape(1, 128)
        qs, kn, vb, cum = _fox_prep_call(proj, _tile_heads(fox_qn_g[l]), _tile_heads(fox_kn_g[l]), fb, e512, T)
        cum_t = jnp.transpose(cum.reshape(B, T, 128)[:, :, 0:GH], (0, 2, 1))
        yd = _fox_call(proj, qs, kn, vb, cum, cum_t, T)

        x2 = _out_call(ya, yb, yc, yd, w_out[l].astype(BF16), x2, gate1, T)
        x2 = _ffn_call(x2, norm2_g[l].reshape(1, D), scale2, shift2, gate2,
                       ffn_w1[l].astype(BF16), ffn_w3[l].astype(BF16), ffn_w2[l].astype(BF16), T)
    return x2.reshape(B, T, D)
```

```python
import functools

import numpy as np
import jax
import jax.numpy as jnp
from jax import lax
from jax.experimental import pallas as pl
from jax.experimental.pallas import tpu as pltpu

F32 = jnp.float32
BF16 = jnp.bfloat16

D_MODEL = 2048
HEAD_DIM = 64
GW = 512
GH = 8
BLOCK = 128
NORM_EPS = 1e-6
LN_EPS = 1e-5
NEG_INF = -1e30
RWKV_GN_EPS = 64e-5
SWA_GROUPS = 4
RWKV_CHUNK = 64
RWKV_CHUNKS_PER_STEP = 4
LOG2E = float(np.log2(np.e))
FFN_HIDDEN = 5632

OFF_FQ, OFF_FK, OFF_FV, OFF_FG = 0, 512, 1024, 1536
OFF_SU, OFF_SV = 2048, 2560
OFF_RR = 3072
OFF_WQ = 4608
OFF_WKV = 5120
OFF_RLR = 5376
OFF_FF = 5632
NP = 5760

VMEM_LIMIT = 56 * 1024 * 1024


def _cparams(sem):
    return pltpu.CompilerParams(dimension_semantics=sem, vmem_limit_bytes=VMEM_LIMIT)


def _dot(a, b):
    return jnp.dot(a, b, preferred_element_type=F32)


def _dot_nt(a, b):
    return lax.dot_general(a, b, (((1,), (1,)), ((), ())), preferred_element_type=F32)


def _dot_tn(a, b):
    return lax.dot_general(a, b, (((0,), (0,)), ((), ())), preferred_element_type=F32)


def _seg_mean(x, e_ref):
    hi = x.astype(BF16)
    lo = (x - hi.astype(F32)).astype(BF16)
    e = e_ref[...]
    return _dot(hi, e) + _dot(lo, e)


def _half_masks(shape):
    lane = lax.broadcasted_iota(jnp.int32, shape, len(shape) - 1)
    m0 = (lane % 128) < 64
    return m0


def _mod_kernel(c_ref, w_ref, b_ref, o_ref):
    c = c_ref[...]
    ca = (c * jax.nn.sigmoid(c)).astype(BF16)
    o_ref[0] = _dot(ca, w_ref[0].astype(BF16)) + b_ref[0]


def _mod_call(c, w_mod, b_mod):
    L, D, N = w_mod.shape
    B = c.shape[0]
    tn = 1024
    return pl.pallas_call(
        _mod_kernel,
        grid=(L, N // tn),
        in_specs=[
            pl.BlockSpec((B, D), lambda l, j: (0, 0)),
            pl.BlockSpec((1, D, tn), lambda l, j: (l, 0, j)),
            pl.BlockSpec((1, 1, tn), lambda l, j: (l, 0, j)),
        ],
        out_specs=pl.BlockSpec((1, B, tn), lambda l, j: (l, 0, j)),
        out_shape=jax.ShapeDtypeStruct((L, B, N), F32),
        compiler_params=_cparams(("parallel", "parallel")),
        name="adaln_mod",
    )(c, w_mod, b_mod.reshape(L, 1, N))


def _in_kernel(x_ref, g_ref, sc_ref, sh_ref, w_ref, o_ref, h_ref):
    @pl.when(pl.program_id(1) == 0)
    def _():
        x = x_ref[...]
        ms = jnp.mean(x * x, axis=-1, keepdims=True)
        y = x * lax.rsqrt(ms + NORM_EPS) * g_ref[...]
        h_ref[...] = (y * (1.0 + sc_ref[0]) + sh_ref[0]).astype(BF16)

    o_ref[...] = _dot(h_ref[...], w_ref[...])


def _in_call(x2, g, scale, shift, w, T):
    M, D = x2.shape
    n = w.shape[1]
    tm, tn = 1024, 1152
    return pl.pallas_call(
        _in_kernel,
        grid=(M // tm, n // tn),
        in_specs=[
            pl.BlockSpec((tm, D), lambda i, j: (i, 0)),
            pl.BlockSpec((1, D), lambda i, j: (0, 0)),
            pl.BlockSpec((1, 1, D), lambda i, j: (i * tm // T, 0, 0)),
            pl.BlockSpec((1, 1, D), lambda i, j: (i * tm // T, 0, 0)),
            pl.BlockSpec((D, tn), lambda i, j: (0, j)),
        ],
        out_specs=pl.BlockSpec((tm, tn), lambda i, j: (i, j)),
        out_shape=jax.ShapeDtypeStruct((M, n), F32),
        scratch_shapes=[pltpu.VMEM((tm, D), BF16)],
        compiler_params=_cparams(("parallel", "arbitrary")),
        name="norm_in_proj",
    )(x2, g, scale, shift, w)


def _out_kernel(ya_ref, yb_ref, yc_ref, yd_ref, w_ref, x_ref, gt_ref, o_ref):
    acc = _dot(ya_ref[...], w_ref[0:GW, :])
    acc += _dot(yb_ref[...], w_ref[GW:2 * GW, :])
    acc += _dot(yc_ref[...], w_ref[2 * GW:3 * GW, :])
    acc += _dot(yd_ref[...], w_ref[3 * GW:4 * GW, :])
    o_ref[...] = x_ref[...] + gt_ref[0] * acc


def _out_call(ya, yb, yc, yd, w, x2, gate, T):
    M, D = x2.shape
    tm = 512
    yspec = pl.BlockSpec((tm, GW), lambda i: (i, 0))
    return pl.pallas_call(
        _out_kernel,
        grid=(M // tm,),
        in_specs=[
            yspec, yspec, yspec, yspec,
            pl.BlockSpec((D, D), lambda i: (0, 0)),
            pl.BlockSpec((tm, D), lambda i: (i, 0)),
            pl.BlockSpec((1, 1, D), lambda i: (i * tm // T, 0, 0)),
        ],
        out_specs=pl.BlockSpec((tm, D), lambda i: (i, 0)),
        out_shape=jax.ShapeDtypeStruct((M, D), F32),
        compiler_params=_cparams(("parallel",)),
        name="out_proj_residual",
    )(ya, yb, yc, yd, w, x2, gate)


def _ffn_kernel(x_ref, g_ref, sc_ref, sh_ref, gt_ref, w1_ref, w3_ref, w2_ref, o_ref, h_ref):
    @pl.when(pl.program_id(1) == 0)
    def _():
        x = x_ref[...]
        ms = jnp.mean(x * x, axis=-1, keepdims=True)
        y = x * lax.rsqrt(ms + NORM_EPS) * g_ref[...]
        h_ref[...] = (y * (1.0 + sc_ref[0]) + sh_ref[0]).astype(BF16)
        o_ref[...] = x

    h = h_ref[...]
    a = _dot(h, w1_ref[...])
    b = _dot(h, w3_ref[...])
    f = (a * jax.nn.sigmoid(a) * b).astype(BF16)
    o_ref[...] += gt_ref[0] * _dot(f, w2_ref[...])


def _ffn_call(x2, g, scale, shift, gate, w1, w3, w2, T):
    M, D = x2.shape
    F = w1.shape[1]
    tm, tf = 512, 512
    bspec = pl.BlockSpec((1, 1, D), lambda i, j: (i * tm // T, 0, 0))
    return pl.pallas_call(
        _ffn_kernel,
        grid=(M // tm, F // tf),
        in_specs=[
            pl.BlockSpec((tm, D), lambda i, j: (i, 0)),
            pl.BlockSpec((1, D), lambda i, j: (0, 0)),
            bspec, bspec, bspec,
            pl.BlockSpec((D, tf), lambda i, j: (0, j)),
            pl.BlockSpec((D, tf), lambda i, j: (0, j)),
            pl.BlockSpec((tf, D), lambda i, j: (j, 0)),
        ],
        out_specs=pl.BlockSpec((tm, D), lambda i, j: (i, 0)),
        out_shape=jax.ShapeDtypeStruct((M, D), F32),
        scratch_shapes=[pltpu.VMEM((tm, D), BF16)],
        compiler_params=_cparams(("parallel", "arbitrary")),
        name="swiglu_ffn",
    )(x2, g, scale, shift, gate, w1, w3, w2)


def _gelu(x):
    return 0.5 * x * (1.0 + lax.erf(x * np.float32(1.0 / np.sqrt(2.0))))


def _sgu_kernel(u_ref, v_ref, g_ref, b_ref, ws_ref, bias_ref, o_ref, *, chunks):
    m0 = _half_masks((BLOCK, 128))
    for c in range(chunks):
        rows = slice(c * BLOCK, (c + 1) * BLOCK)
        u = _gelu(u_ref[rows, :])
        v = _gelu(v_ref[rows, :])
        mu = jnp.mean(v, axis=-1, keepdims=True)
        vc = v - mu
        var = jnp.mean(vc * vc, axis=-1, keepdims=True)
        vn = vc * lax.rsqrt(var + LN_EPS) * g_ref[...] + b_ref[...]
        outs = []
        for j in range(GH // 2):
            vp = vn[:, 128 * j:128 * (j + 1)]
            stacked = jnp.concatenate(
                [jnp.where(m0, vp, 0.0), jnp.where(m0, 0.0, vp)], axis=0).astype(BF16)
            outs.append(_dot(ws_ref[j], stacked))
        z = jnp.concatenate(outs, axis=1) + bias_ref[...]
        o_ref[rows, :] = (u * z).astype(BF16)


def _sgu_call(proj, ln_g, ln_b, ws_cat, bias_full):
    M = proj.shape[0]
    chunks = 4
    tr = chunks * BLOCK
    return pl.pallas_call(
        functools.partial(_sgu_kernel, chunks=chunks),
        grid=(M // tr,),
        in_specs=[
            pl.BlockSpec((tr, GW), lambda i: (i, OFF_SU // GW)),
            pl.BlockSpec((tr, GW), lambda i: (i, OFF_SV // GW)),
            pl.BlockSpec((1, GW), lambda i: (0, 0)),
            pl.BlockSpec((1, GW), lambda i: (0, 0)),
            pl.BlockSpec((GH // 2, BLOCK, 2 * BLOCK), lambda i: (0, 0, 0)),
            pl.BlockSpec((BLOCK, GW), lambda i: (0, 0)),
        ],
        out_specs=pl.BlockSpec((tr, GW), lambda i: (i, 0)),
        out_shape=jax.ShapeDtypeStruct((M, GW), BF16),
        compiler_params=_cparams(("parallel",)),
        name="sgu_mix",
    )(proj, proj, ln_g, ln_b, ws_cat, bias_full)


def _swa_kernel(sink_ref, q_ref, kv_ref, kvp_ref, qg_ref, kg_ref, e_ref, o_ref, *, blocks, blocks_per_seq):
    i = pl.program_id(0)
    m0q = _half_masks((BLOCK, 128))
    m0k = _half_masks((2 * BLOCK, 128))
    row = lax.broadcasted_iota(jnp.int32, (BLOCK, 2 * BLOCK), 0)
    col = lax.broadcasted_iota(jnp.int32, (BLOCK, 2 * BLOCK), 1)
    band = (col <= row + BLOCK) & (col > row)
    e128 = e_ref.at[0:128, 0:128]

    kv_all = jnp.concatenate([kvp_ref[...], kv_ref[...]], axis=0)
    for r in range(blocks):
        first = ((i * blocks + r) % blocks_per_seq) == 0
        mask = band & (col >= jnp.where(first, BLOCK, 0))
        q = q_ref[r * BLOCK:(r + 1) * BLOCK, :]
        qn = q * lax.rsqrt(_seg_mean(q * q, e_ref) + NORM_EPS) * qg_ref[...]
        kv = kv_all[r * BLOCK:(r + 2) * BLOCK, :]
        k = kv[:, 0:128]
        v = kv[:, 128:256]
        kn = k * lax.rsqrt(_seg_mean(k * k, e128) + NORM_EPS) * kg_ref[...]
        kn_sw = pltpu.roll(kn, 64, axis=1)
        v_sw = pltpu.roll(v, 64, axis=1)
        kdup = [jnp.where(m0k, kn, kn_sw).astype(BF16), jnp.where(m0k, kn_sw, kn).astype(BF16)]
        vdup = [jnp.where(m0k, v, v_sw), jnp.where(m0k, v_sw, v)]
        vmsk = [[jnp.where(m0k, vd, 0.0).astype(BF16), jnp.where(m0k, 0.0, vd).astype(BF16)] for vd in vdup]
        scores = []
        for h in range(GH):
            qp = qn[:, 128 * (h // 2):128 * (h // 2 + 1)]
            qm = (jnp.where(m0q, qp, 0.0) if h % 2 == 0 else jnp.where(m0q, 0.0, qp)).astype(BF16)
            scores.append(_dot_nt(qm, kdup[h // SWA_GROUPS]))
        probs = []
        for h in range(GH):
            s = jnp.where(mask, scores[h] * np.float32(HEAD_DIM ** -0.5), NEG_INF)
            sink = sink_ref[h]
            m = jnp.maximum(jnp.max(s, axis=-1, keepdims=True), sink)
            p = jnp.exp(s - m)
            den = jnp.sum(p, axis=-1, keepdims=True) + jnp.exp(sink - m)
            probs.append((p / den).astype(BF16))
        outs = []
        for j in range(GH // 2):
            g = (2 * j) // SWA_GROUPS
            outs.append(_dot(probs[2 * j], vmsk[g][0]) + _dot(probs[2 * j + 1], vmsk[g][1]))
        o_ref[r * BLOCK:(r + 1) * BLOCK, :] = jnp.concatenate(outs, axis=1).astype(BF16)


def _swa_call(proj, sinks, qg, kg, e512, T):
    M = proj.shape[0]
    blocks = 2
    tr = blocks * BLOCK
    return pl.pallas_call(
        functools.partial(_swa_kernel, blocks=blocks, blocks_per_seq=T // BLOCK),
        grid=(M // tr,),
        in_specs=[
            pl.BlockSpec(memory_space=pltpu.SMEM),
            pl.BlockSpec((tr, GW), lambda i: (i, OFF_WQ // GW)),
            pl.BlockSpec((tr, 256), lambda i: (i, OFF_WKV // 256)),
            pl.BlockSpec((BLOCK, 256), lambda i: (jnp.maximum(i * blocks - 1, 0), OFF_WKV // 256)),
            pl.BlockSpec((1, GW), lambda i: (0, 0)),
            pl.BlockSpec((1, 128), lambda i: (0, 0)),
            pl.BlockSpec((GW, GW), lambda i: (0, 0)),
        ],
        out_specs=pl.BlockSpec((tr, GW), lambda i: (i, 0)),
        out_shape=jax.ShapeDtypeStruct((M, GW), BF16),
        compiler_params=_cparams(("parallel",)),
        name="swa_mix",
    )(sinks, proj, proj, proj, qg, kg, e512)


def _fox_prep_kernel(q_ref, k_ref, v_ref, f_ref, qg_ref, kg_ref, fb_ref, e_ref, tri_ref, sel_ref, ones_ref,
                     qt_ref, ka_ref, vt_ref, carry_ref):
    @pl.when(pl.program_id(1) == 0)
    def _():
        carry_ref[...] = jnp.zeros_like(carry_ref)

    q = q_ref[...]
    k = k_ref[...]
    v = v_ref[...]
    tr = q.shape[0]
    qn = q * lax.rsqrt(_seg_mean(q * q, e_ref) + NORM_EPS) * qg_ref[...] * np.float32(HEAD_DIM ** -0.5 * LOG2E)
    kn = k * lax.rsqrt(_seg_mean(k * k, e_ref) + NORM_EPS) * kg_ref[...]
    z = f_ref[...] + fb_ref[...]
    logf2 = (jnp.minimum(z, 0.0) - jnp.log1p(jnp.exp(-jnp.abs(z)))) * np.float32(LOG2E)
    cum = jnp.dot(tri_ref[...], logf2, preferred_element_type=F32,
                  precision=lax.Precision.HIGHEST) + carry_ref[...]
    carry_ref[...] = cum[tr - 1:tr, :]
    neg = -cum
    hi = neg.astype(BF16)
    r1 = neg - hi.astype(F32)
    mid = r1.astype(BF16)
    lo = (r1 - mid.astype(F32)).astype(BF16)
    bias = _dot(jnp.concatenate([hi, mid, lo], axis=1), sel_ref[...])
    m0 = _half_masks((tr, 128))
    for h in range(GH):
        p, half = divmod(h, 2)
        slot = slice(128 * h, 128 * (h + 1))
        pair = slice(128 * p, 128 * (p + 1))
        own_k, own_q, other_k, other_q = kn[:, pair], qn[:, pair], bias[:, slot], ones_ref[:, slot]
        if half == 0:
            ka = jnp.where(m0, own_k, other_k)
            qa = jnp.where(m0, own_q, other_q)
        else:
            ka = jnp.where(m0, other_k, own_k)
            qa = jnp.where(m0, other_q, own_q)
        ka_ref[:, slot] = ka.astype(BF16)
        qt_ref[0, slot, :] = qa.T.astype(BF16)
    for p in range(GH // 2):
        pair = slice(128 * p, 128 * (p + 1))
        vt_ref[0, pair, :] = v[:, pair].T.astype(BF16)


def _fox_prep_call(proj, qg, kg, fb, e512, T):
    M = proj.shape[0]
    B = M // T
    tr = 256
    nb = T // tr
    tri = jnp.tril(jnp.ones((tr, tr), F32))
    sel = np.zeros((3 * 128, GH * 128), np.float32)
    ones = np.zeros((1, GH * 128), np.float32)
    for h in range(GH):
        off = 128 * h + (64 if h % 2 == 0 else 0)
        for part in range(3):
            sel[128 * part + h, off + part] = 1.0
            ones[0, off + part] = 1.0
    cspec = lambda c: pl.BlockSpec((tr, GW), lambda b, i, c=c: (b * nb + i, c))
    return pl.pallas_call(
        _fox_prep_kernel,
        grid=(B, nb),
        in_specs=[
            cspec(OFF_FQ // GW), cspec(OFF_FK // GW), cspec(OFF_FV // GW),
            pl.BlockSpec((tr, 128), lambda b, i: (b * nb + i, OFF_FF // 128)),
            pl.BlockSpec((1, GW), lambda b, i: (0, 0)),
            pl.BlockSpec((1, GW), lambda b, i: (0, 0)),
            pl.BlockSpec((1, 128), lambda b, i: (0, 0)),
            pl.BlockSpec((GW, GW), lambda b, i: (0, 0)),
            pl.BlockSpec((tr, tr), lambda b, i: (0, 0)),
            pl.BlockSpec((3 * 128, GH * 128), lambda b, i: (0, 0)),
            pl.BlockSpec((1, GH * 128), lambda b, i: (0, 0)),
        ],
        out_specs=[
            pl.BlockSpec((1, GH * 128, tr), lambda b, i: (b, 0, i)),
            pl.BlockSpec((tr, GH * 128), lambda b, i: (b * nb + i, 0)),
            pl.BlockSpec((1, GW, tr), lambda b, i: (b, 0, i)),
        ],
        out_shape=[
            jax.ShapeDtypeStruct((B, GH * 128, T), BF16),
            jax.ShapeDtypeStruct((M, GH * 128), BF16),
            jax.ShapeDtypeStruct((B, GW, T), BF16),
        ],
        scratch_shapes=[pltpu.VMEM((1, 128), F32)],
        compiler_params=_cparams(("parallel", "arbitrary")),
        name="fox_prep",
    )(proj, proj, proj, proj, qg, kg, fb, e512, tri, jnp.asarray(sel, BF16), jnp.asarray(ones, F32))


def _fox_kernel(qt_ref, ka_ref, vt_ref, gl_ref, o_ref, m_ref, l_ref, acc_ref, *, tq):
    i = pl.program_id(1)
    m_ref[...] = jnp.full(m_ref.shape, NEG_INF, F32)
    l_ref[...] = jnp.zeros(l_ref.shape, F32)
    acc_ref[...] = jnp.zeros(acc_ref.shape, F32)

    def step(kb, masked):
        start = pl.multiple_of(kb * tq, tq)
        if masked:
            key = lax.broadcasted_iota(jnp.int32, (tq, tq), 0)
            qry = lax.broadcasted_iota(jnp.int32, (tq, tq), 1)
            keep = key <= qry
        sts = []
        for h in range(GH):
            slot = slice(128 * h, 128 * (h + 1))
            sts.append(_dot(ka_ref[pl.ds(start, tq), slot], qt_ref[0, slot, :]))
        pts, alphas = [], []
        for h in range(GH):
            st = jnp.where(keep, sts[h], NEG_INF) if masked else sts[h]
            m_old = m_ref[h]
            m_new = jnp.maximum(m_old, jnp.max(st, axis=0, keepdims=True))
            alpha = jnp.exp2(m_old - m_new)
            pt = jnp.exp2(st - m_new)
            l_ref[h] = alpha * l_ref[h] + jnp.sum(pt, axis=0, keepdims=True)
            m_ref[h] = m_new
            pts.append(pt.astype(BF16))
            alphas.append(alpha)
        for h in range(GH):
            vth = vt_ref[0, 64 * h:64 * (h + 1), pl.ds(start, tq)]
            acc_ref[h] = alphas[h] * acc_ref[h] + _dot(vth, pts[h])

    def body(kb, carry):
        step(kb, False)
        return carry

    lax.fori_loop(0, i, body, 0)
    step(i, True)
    for p in range(GH // 2):
        pair = slice(128 * p, 128 * (p + 1))
        ot = jnp.concatenate([acc_ref[2 * p] / l_ref[2 * p], acc_ref[2 * p + 1] / l_ref[2 * p + 1]], axis=0)
        o_ref[:, pair] = (ot.T * jax.nn.sigmoid(gl_ref[:, pair])).astype(BF16)


def _fox_call(proj, qt, ka, vt, T):
    M = proj.shape[0]
    B = M // T
    tq = 256
    nq = T // tq
    return pl.pallas_call(
        functools.partial(_fox_kernel, tq=tq),
        grid=(B, nq),
        in_specs=[
            pl.BlockSpec((1, GH * 128, tq), lambda b, i: (b, 0, i)),
            pl.BlockSpec((T, GH * 128), lambda b, i: (b, 0)),
            pl.BlockSpec((1, GW, T), lambda b, i: (b, 0, 0)),
            pl.BlockSpec((tq, GW), lambda b, i: (b * nq + i, OFF_FG // GW)),
        ],
        out_specs=pl.BlockSpec((tq, GW), lambda b, i: (b * nq + i, 0)),
        out_shape=jax.ShapeDtypeStruct((M, GW), BF16),
        scratch_shapes=[
            pltpu.VMEM((GH, 1, tq), F32),
            pltpu.VMEM((GH, 1, tq), F32),
            pltpu.VMEM((GH, HEAD_DIM, tq), F32),
        ],
        compiler_params=_cparams(("parallel", "arbitrary")),
        name="fox_attn",
    )(qt, ka, vt, proj)


def _rwkv_local_kernel(p_ref, lr_ref, pp_ref, lrp_ref, mu_ref, mulr_ref, wlr_ref, w0_ref, a0_ref,
                       kk_ref, ka_ref, rk_ref, e_ref, tri_ref,
                       m_ref, n_ref, r_ref, y0_ref, bonus_ref, g_ref, *, chunks_per_seq, nc):
    C = RWKV_CHUNK
    i = pl.program_id(0)
    has_prev = (((i * nc) % chunks_per_seq) != 0).astype(F32)
    row = lax.broadcasted_iota(jnp.int32, (nc * C, 1), 0)

    def shifted(cur_ref, prev_ref, mu):
        cur = cur_ref[...]
        prev_row = prev_ref[7:8, :] * has_prev
        prev = jnp.where(row == 0, prev_row, pltpu.roll(cur, 1, axis=0))
        return cur + (prev - cur) * mu

    p = shifted(p_ref, pp_ref, mu_ref[...])
    lr = shifted(lr_ref, lrp_ref, mulr_ref[...])
    r = p[:, 0:GW]
    k = p[:, GW:2 * GW]
    v = p[:, 2 * GW:3 * GW]

    lane = lax.broadcasted_iota(jnp.int32, lr.shape, 1)
    z = jnp.where(lane < 32, jnp.tanh(lr), jnp.where(lane < 64, lr, jax.nn.sigmoid(lr)))
    low = _dot(z.astype(BF16), wlr_ref[...])
    x = -(w0_ref[...] + low[:, 0:GW])
    softplus = jnp.maximum(x, 0.0) + jnp.log1p(jnp.exp(-jnp.abs(x)))
    w = -softplus - 0.5
    lw = -jnp.exp(w)
    a = jax.nn.sigmoid(a0_ref[...] + low[:, GW:2 * GW])
    g_ref[...] = low[:, 2 * GW:3 * GW]

    kk = k * kk_ref[...]
    n2 = _seg_mean(kk * kk, e_ref) * np.float32(HEAD_DIM)
    kk = kk / jnp.maximum(jnp.sqrt(n2), 1e-12)
    k = k * (1.0 + (a - 1.0) * ka_ref[...])
    b = kk * a
    bonus_ref[...] = _seg_mean(r * k * rk_ref[...], e_ref) * np.float32(HEAD_DIM) * v

    tri = tri_ref[...]
    lw_hi = lw.astype(BF16)
    lw_lo = (lw - lw_hi.astype(F32)).astype(BF16)
    L_all = _dot(tri, lw_hi) + _dot(tri, lw_lo)

    m0 = _half_masks((C, 128))
    ri = lax.broadcasted_iota(jnp.int32, (2 * C, 2 * C), 0)
    ci = lax.broadcasted_iota(jnp.int32, (2 * C, 2 * C), 1)
    strict = (ri % C) > (ci % C)
    incl = (ri % C) >= (ci % C)
    eye = ri == ci

    def stack(xp):
        return jnp.concatenate([jnp.where(m0, xp, 0.0), jnp.where(m0, 0.0, xp)], axis=0)

    chains = [(c, j) for c in range(nc) for j in range(GH // 2)]
    st = []
    for c, j in chains:
        rs = slice(c * C, (c + 1) * C)
        sl = slice(128 * j, 128 * (j + 1))
        L = L_all[rs, sl]
        Lend = L[C - 1:C, :]
        e_pos = jnp.exp(L)
        e_neg = jnp.exp(-L)
        e_hat = jnp.exp(Lend - L)
        rc, kc, bc = r[rs, sl], k[rs, sl], b[rs, sl]
        rst = stack(rc * e_pos)
        ast_b = stack(-kk[rs, sl] * jnp.exp(L - lw[rs, sl])).astype(BF16)
        st.append(dict(
            rs=rs, sl=sl, rst=rst, ast_b=ast_b, vst_b=stack(v[rs, sl]).astype(BF16),
            left=jnp.concatenate([ast_b, rst.astype(BF16)], axis=0),
            right=jnp.concatenate([stack(bc * e_neg), stack(kc * e_neg)], axis=0).astype(BF16),
            bkh=jnp.concatenate([stack(bc * e_hat), stack(kc * e_hat)], axis=0).astype(BF16),
            wc=jnp.exp(Lend)))
    for s in st:
        G = _dot_nt(s["left"], s["right"])
        a_ab = jnp.where(strict, G[0:2 * C, 0:2 * C], 0.0)
        s["a_ak"] = jnp.where(strict, G[0:2 * C, 2 * C:4 * C], 0.0).astype(BF16)
        s["a_r"] = jnp.concatenate([jnp.where(incl, G[2 * C:4 * C, 0:2 * C], 0.0),
                                    jnp.where(incl, G[2 * C:4 * C, 2 * C:4 * C], 0.0)], axis=1).astype(BF16)
        s["t_inv"] = jnp.where(eye, 1.0, a_ab)
        s["apow"] = a_ab.astype(BF16)
    for s in st:
        s["akv"] = _dot(s["a_ak"], s["vst_b"]).astype(BF16)
    for _ in range(5):
        for s in st:
            s["apow"] = _dot(s["apow"], s["apow"]).astype(BF16)
        for s in st:
            s["t_inv"] = s["t_inv"] + _dot(s["apow"], s["t_inv"].astype(BF16))
    for s in st:
        pq = _dot(s["t_inv"].astype(BF16), jnp.concatenate([s["ast_b"], s["akv"]], axis=1))
        s["rhs2"] = jnp.concatenate(
            [pq.astype(BF16), jnp.concatenate([jnp.zeros_like(s["vst_b"]), s["vst_b"]], axis=1)], axis=0)
    for s in st:
        ry = _dot(s["a_r"], s["rhs2"])
        rhat = s["rst"] + ry[:, 0:128]
        y0 = ry[:, 128:256]
        r_ref[s["rs"], s["sl"]] = rhat[0:C, :] + rhat[C:2 * C, :]
        y0_ref[s["rs"], s["sl"]] = y0[0:C, :] + y0[C:2 * C, :]
    for (c, j), s in zip(chains, st):
        mn = _dot_tn(s["bkh"], s["rhs2"])
        m_ref[c, j] = mn[:, 0:128] + jnp.where(eye, s["wc"], 0.0)
        n_ref[c, j] = mn[:, 128:256]


def _rwkv_local_call(proj, mu_rkv, mu_lr, wlr, w0, a0, kk, ka, rk, e512, T):
    M = proj.shape[0]
    C = RWKV_CHUNK
    nc = RWKV_CHUNKS_PER_STEP
    R = nc * C
    nchunks = M // C
    ridx = jnp.arange(R)
    tri = ((ridx[:, None] >= ridx[None, :]) & (ridx[:, None] // C == ridx[None, :] // C)).astype(BF16)
    vec = lambda n: pl.BlockSpec((1, n), lambda i: (0, 0))
    rowspec = pl.BlockSpec((R, GW), lambda i: (i, 0))
    matspec = pl.BlockSpec((nc, GH // 2, 128, 128), lambda i: (i, 0, 0, 0))
    prev = lambda i: jnp.maximum(i * (R // 8) - 1, 0)
    return pl.pallas_call(
        functools.partial(_rwkv_local_kernel, chunks_per_seq=T // C, nc=nc),
        grid=(nchunks // nc,),
        in_specs=[
            pl.BlockSpec((R, 3 * GW), lambda i: (i, OFF_RR // (3 * GW))),
            pl.BlockSpec((R, 256), lambda i: (i, OFF_RLR // 256)),
            pl.BlockSpec((8, 3 * GW), lambda i: (prev(i), OFF_RR // (3 * GW))),
            pl.BlockSpec((8, 256), lambda i: (prev(i), OFF_RLR // 256)),
            vec(3 * GW), vec(256),
            pl.BlockSpec((256, 3 * GW), lambda i: (0, 0)),
            vec(GW), vec(GW), vec(GW), vec(GW), vec(GW),
            pl.BlockSpec((GW, GW), lambda i: (0, 0)),
            pl.BlockSpec((R, R), lambda i: (0, 0)),
        ],
        out_specs=[matspec, matspec, rowspec, rowspec, rowspec, rowspec],
        out_shape=[
            jax.ShapeDtypeStruct((nchunks, GH // 2, 128, 128), F32),
            jax.ShapeDtypeStruct((nchunks, GH // 2, 128, 128), F32),
            jax.ShapeDtypeStruct((M, GW), F32),
            jax.ShapeDtypeStruct((M, GW), F32),
            jax.ShapeDtypeStruct((M, GW), F32),
            jax.ShapeDtypeStruct((M, GW), F32),
        ],
        compiler_params=_cparams(("parallel",)),
        name="rwkv_chunk_local",
    )(proj, proj, proj, proj, mu_rkv, mu_lr, wlr, w0, a0, kk, ka, rk, e512, tri)


def _rwkv_state_kernel(m_ref, n_ref, r_ref, y0_ref, bonus_ref, g_ref, lg_ref, lb_ref, e_ref,
                       o_ref, h_ref, *, batch):
    @pl.when(pl.program_id(0) == 0)
    def _():
        h_ref[...] = jnp.zeros_like(h_ref)

    for b in range(batch):
        ys = []
        for j in range(GH // 2):
            sl = slice(128 * j, 128 * (j + 1))
            h = h_ref[b, j].astype(BF16)
            ys.append(_dot(r_ref[b, :, sl].astype(BF16), h) + y0_ref[b, :, sl])
            h_ref[b, j] = _dot(m_ref[b, 0, j].astype(BF16), h) + n_ref[b, 0, j]
        y = jnp.concatenate(ys, axis=1)
        mean = _seg_mean(y, e_ref)
        yc = y - mean
        var = _seg_mean(yc * yc, e_ref)
        yn = yc * lax.rsqrt(var + RWKV_GN_EPS) * lg_ref[...] + lb_ref[...]
        o_ref[b] = ((yn + bonus_ref[b]) * g_ref[b]).astype(BF16)


def _rwkv_state_call(mm, nn, rr, y0, bonus, g, lnx_g, lnx_b, e512, B, T):
    C = RWKV_CHUNK
    nc = T // C
    mm = mm.reshape(B, nc, GH // 2, 128, 128)
    nn = nn.reshape(B, nc, GH // 2, 128, 128)
    r3 = lambda a: a.reshape(B, T, GW)
    matspec = pl.BlockSpec((B, 1, GH // 2, 128, 128), lambda c: (0, c, 0, 0, 0))
    rowspec = pl.BlockSpec((B, C, GW), lambda c: (0, c, 0))
    vec = pl.BlockSpec((1, GW), lambda c: (0, 0))
    out = pl.pallas_call(
        functools.partial(_rwkv_state_kernel, batch=B),
        grid=(nc,),
        in_specs=[matspec, matspec, rowspec, rowspec, rowspec, rowspec, vec, vec,
                  pl.BlockSpec((GW, GW), lambda c: (0, 0))],
        out_specs=rowspec,
        out_shape=jax.ShapeDtypeStruct((B, T, GW), BF16),
        scratch_shapes=[pltpu.VMEM((B, GH // 2, 128, 128), F32)],
        compiler_params=_cparams(("arbitrary",)),
        name="rwkv_state_scan",
    )(mm, nn, r3(rr), r3(y0), r3(bonus), r3(g), lnx_g, lnx_b, e512)
    return out.reshape(B * T, GW)


def _pad_cols(a, n):
    return jnp.pad(a, [(0, 0)] * (a.ndim - 1) + [(0, n - a.shape[-1])])


def _permute_in_cols(w):
    rw, sw, sg, fx = 0, 1696, 2464, 3488
    parts = [
        w[..., fx:fx + 1536],
        w[..., fx + 1544:fx + 2056],
        w[..., sg:sg + 1024],
        w[..., rw:rw + 1536],
        w[..., sw:sw + 768],
        _pad_cols(w[..., rw + 1536:rw + 1696], 256),
        _pad_cols(w[..., fx + 1536:fx + 1544], 128),
    ]
    return jnp.concatenate(parts, axis=-1)


def _tile_heads(g):
    return jnp.tile(g, GH).reshape(1, GW)


def kernel(x, c, w_mod, b_mod, norm1_g, norm2_g, w_in, w_out, rwkv_mu, rwkv_w0, rwkv_w2, rwkv_a0, rwkv_a2, rwkv_g2, rwkv_kk, rwkv_ka, rwkv_rk, rwkv_lnx_g, rwkv_lnx_b, swa_qn_g, swa_kn_g, swa_sinks, sgu_ln_g, sgu_ln_b, sgu_ws, sgu_b, fox_qn_g, fox_kn_g, fox_fb, ffn_w1, ffn_w3, ffn_w2):
    B, T, D = x.shape
    L = w_mod.shape[0]
    M = B * T

    mod = _mod_call(c, w_mod, b_mod)
    mod = mod.reshape(L, B, 6, 1, D)

    seg = jnp.arange(GW) // HEAD_DIM
    e512 = jnp.where(seg[:, None] == seg[None, :], 1.0 / HEAD_DIM, 0.0).astype(BF16)
    rows, cols = np.tril_indices(BLOCK)

    x2 = x.reshape(M, D)
    for l in range(L):
        shift1, scale1, gate1, shift2, scale2, gate2 = [mod[l, :, s] for s in range(6)]
        w_in_l = _permute_in_cols(w_in[l]).astype(BF16)
        proj = _in_call(x2, norm1_g[l].reshape(1, D), scale1, shift1, w_in_l, T)

        mu = rwkv_mu[l]
        mu_rkv = mu[0:3 * GW].reshape(1, 3 * GW)
        mu_lr = _pad_cols(mu[3 * GW:], 256).reshape(1, 256)
        wlr = jnp.zeros((256, 3 * GW), F32)
        wlr = wlr.at[0:32, 0:GW].set(rwkv_w2[l])
        wlr = wlr.at[32:64, GW:2 * GW].set(rwkv_a2[l])
        wlr = wlr.at[64:160, 2 * GW:3 * GW].set(rwkv_g2[l])
        vec = lambda a: a.reshape(1, GW)
        mm, nn, rr, y0, bonus, g = _rwkv_local_call(
            proj, mu_rkv, mu_lr, wlr.astype(BF16), vec(rwkv_w0[l]), vec(rwkv_a0[l]), vec(rwkv_kk[l]),
            vec(rwkv_ka[l]), vec(rwkv_rk[l]), e512, T)
        ya = _rwkv_state_call(mm, nn, rr, y0, bonus, g, vec(rwkv_lnx_g[l]), vec(rwkv_lnx_b[l]), e512, B, T)

        yb = _swa_call(proj, swa_sinks[l], _tile_heads(swa_qn_g[l]), jnp.tile(swa_kn_g[l], 2).reshape(1, 128),
                       e512, T)

        w_s = jnp.zeros((GH, BLOCK, BLOCK), F32).at[:, rows, cols].set(sgu_ws[l])
        ws_cat = jnp.concatenate([w_s[0::2], w_s[1::2]], axis=2).astype(BF16)
        bias_full = jnp.repeat(jnp.transpose(sgu_b[l]), HEAD_DIM, axis=1)
        yc = _sgu_call(proj, vec(sgu_ln_g[l]), vec(sgu_ln_b[l]), ws_cat, bias_full)

        fb = _pad_cols(fox_fb[l], 128).reshape(1, 128)
        qt, ka, vt = _fox_prep_call(proj, _tile_heads(fox_qn_g[l]), _tile_heads(fox_kn_g[l]), fb, e512, T)
        yd = _fox_call(proj, qt, ka, vt, T)

        x2 = _out_call(ya, yb, yc, yd, w_out[l].astype(BF16), x2, gate1, T)
        x2 = _ffn_call(x2, norm2_g[l].reshape(1, D), scale2, shift2, gate2,
                       ffn_w1[l].astype(BF16), ffn_w3[l].astype(BF16), ffn_w2[l].astype(BF16), T)
    return x2.reshape(B, T, D)
```

```python
import functools

import numpy as np
import jax
import jax.numpy as jnp
from jax import lax
from jax.experimental import pallas as pl
from jax.experimental.pallas import tpu as pltpu

F32 = jnp.float32
BF16 = jnp.bfloat16

D_MODEL = 2048
HEAD_DIM = 64
GW = 512
GH = 8
BLOCK = 128
NORM_EPS = 1e-6
LN_EPS = 1e-5
NEG_INF = -1e30
RWKV_GN_EPS = 64e-5
NORM_ROWS = 16
SWA_GROUPS = 4
RWKV_CHUNK = 64
RWKV_CHUNKS_PER_STEP = 4
RWKV_CHUNKS_PER_GROUP = 4
RWKV_STATE_CHUNKS_PER_STEP = 2
LOG2E = float(np.log2(np.e))
FFN_HIDDEN = 5632

OFF_FQ, OFF_FK, OFF_FV, OFF_FG = 0, 512, 1024, 1536
OFF_SU, OFF_SV = 2048, 2560
OFF_RR = 3072
OFF_WQ = 4608
OFF_WKV = 5120
OFF_RLR = 5376
OFF_FF = 5632
NP = 5760

VMEM_LIMIT = 56 * 1024 * 1024


def _cparams(sem):
    return pltpu.CompilerParams(dimension_semantics=sem, vmem_limit_bytes=VMEM_LIMIT)


def _dot(a, b):
    return jnp.dot(a, b, preferred_element_type=F32)


def _dot_nt(a, b):
    return lax.dot_general(a, b, (((1,), (1,)), ((), ())), preferred_element_type=F32)


def _dot_tn(a, b):
    return lax.dot_general(a, b, (((0,), (0,)), ((), ())), preferred_element_type=F32)


def _seg_mean(x, e_ref):
    return _dot(x.astype(BF16), e_ref[...])


def _half_masks(shape):
    lane = lax.broadcasted_iota(jnp.int32, shape, len(shape) - 1)
    m0 = (lane % 128) < 64
    return m0


def _mod_kernel(c_ref, w_ref, b_ref, o_ref):
    c = c_ref[...]
    ca = (c * jax.nn.sigmoid(c)).astype(BF16)
    o_ref[0] = _dot(ca, w_ref[0].astype(BF16)) + b_ref[0]


def _mod_call(c, w_mod, b_mod):
    L, D, N = w_mod.shape
    B = c.shape[0]
    tn = 1024
    return pl.pallas_call(
        _mod_kernel,
        grid=(L, N // tn),
        in_specs=[
            pl.BlockSpec((B, D), lambda l, j: (0, 0)),
            pl.BlockSpec((1, D, tn), lambda l, j: (l, 0, j)),
            pl.BlockSpec((1, 1, tn), lambda l, j: (l, 0, j)),
        ],
        out_specs=pl.BlockSpec((1, B, tn), lambda l, j: (l, 0, j)),
        out_shape=jax.ShapeDtypeStruct((L, B, N), F32),
        compiler_params=_cparams(("parallel", "parallel")),
        name="adaln_mod",
    )(c, w_mod, b_mod.reshape(L, 1, N))


def _norm_modulate(x_ref, g_ref, sc_ref, sh_ref, h_ref):
    gs = g_ref[...] * (1.0 + sc_ref[0])
    sh = sh_ref[0]

    def body(c, carry):
        rows = pl.ds(pl.multiple_of(c * NORM_ROWS, NORM_ROWS), NORM_ROWS)
        x = x_ref[rows, :]
        ms = jnp.mean(x * x, axis=-1, keepdims=True)
        h_ref[rows, :] = (x * lax.rsqrt(ms + NORM_EPS) * gs + sh).astype(BF16)
        return carry

    lax.fori_loop(0, x_ref.shape[0] // NORM_ROWS, body, 0, unroll=8)


def _in_kernel(x_ref, g_ref, sc_ref, sh_ref, w_ref, o_ref, h_ref):
    @pl.when(pl.program_id(1) == 0)
    def _():
        _norm_modulate(x_ref, g_ref, sc_ref, sh_ref, h_ref)

    o_ref[...] = _dot(h_ref[...], w_ref[...])


def _in_call(x2, g, scale, shift, w, T):
    M, D = x2.shape
    n = w.shape[1]
    tm, tn = 1024, 1920
    return pl.pallas_call(
        _in_kernel,
        grid=(M // tm, n // tn),
        in_specs=[
            pl.BlockSpec((tm, D), lambda i, j: (i, 0)),
            pl.BlockSpec((1, D), lambda i, j: (0, 0)),
            pl.BlockSpec((1, 1, D), lambda i, j: (i * tm // T, 0, 0)),
            pl.BlockSpec((1, 1, D), lambda i, j: (i * tm // T, 0, 0)),
            pl.BlockSpec((D, tn), lambda i, j: (0, j)),
        ],
        out_specs=pl.BlockSpec((tm, tn), lambda i, j: (i, j)),
        out_shape=jax.ShapeDtypeStruct((M, n), F32),
        scratch_shapes=[pltpu.VMEM((tm, D), BF16)],
        compiler_params=_cparams(("parallel", "arbitrary")),
        name="norm_in_proj",
    )(x2, g, scale, shift, w)


def _out_kernel(ya_ref, yb_ref, yc_ref, yd_ref, w_ref, x_ref, gt_ref, o_ref):
    acc = _dot(ya_ref[...], w_ref[0:GW, :])
    acc += _dot(yb_ref[...], w_ref[GW:2 * GW, :])
    acc += _dot(yc_ref[...], w_ref[2 * GW:3 * GW, :])
    acc += _dot(yd_ref[...], w_ref[3 * GW:4 * GW, :])
    o_ref[...] = x_ref[...] + gt_ref[0] * acc


def _out_call(ya, yb, yc, yd, w, x2, gate, T):
    M, D = x2.shape
    tm = 512
    yspec = pl.BlockSpec((tm, GW), lambda i: (i, 0))
    return pl.pallas_call(
        _out_kernel,
        grid=(M // tm,),
        in_specs=[
            yspec, yspec, yspec, yspec,
            pl.BlockSpec((D, D), lambda i: (0, 0)),
            pl.BlockSpec((tm, D), lambda i: (i, 0)),
            pl.BlockSpec((1, 1, D), lambda i: (i * tm // T, 0, 0)),
        ],
        out_specs=pl.BlockSpec((tm, D), lambda i: (i, 0)),
        out_shape=jax.ShapeDtypeStruct((M, D), F32),
        compiler_params=_cparams(("parallel",)),
        name="out_proj_residual",
    )(ya, yb, yc, yd, w, x2, gate)


def _ffn_kernel(x_ref, g_ref, sc_ref, sh_ref, gt_ref, w1_ref, w3_ref, w2_ref, o_ref, h_ref):
    @pl.when(pl.program_id(1) == 0)
    def _():
        _norm_modulate(x_ref, g_ref, sc_ref, sh_ref, h_ref)
        o_ref[...] = x_ref[...]

    h = h_ref[...]
    half = w1_ref.shape[1] // 2
    fs = []
    for c in range(2):
        cols = slice(c * half, (c + 1) * half)
        a = _dot(h, w1_ref[:, cols])
        b = _dot(h, w3_ref[:, cols])
        fs.append((a * jax.nn.sigmoid(a) * b).astype(BF16))
    o_ref[...] += gt_ref[0] * _dot(jnp.concatenate(fs, axis=1), w2_ref[...])


def _ffn_call(x2, g, scale, shift, gate, w1, w3, w2, T):
    M, D = x2.shape
    F = w1.shape[1]
    tm, tf = 1024, 512
    bspec = pl.BlockSpec((1, 1, D), lambda i, j: (i * tm // T, 0, 0))
    return pl.pallas_call(
        _ffn_kernel,
        grid=(M // tm, F // tf),
        in_specs=[
            pl.BlockSpec((tm, D), lambda i, j: (i, 0)),
            pl.BlockSpec((1, D), lambda i, j: (0, 0)),
            bspec, bspec, bspec,
            pl.BlockSpec((D, tf), lambda i, j: (0, j)),
            pl.BlockSpec((D, tf), lambda i, j: (0, j)),
            pl.BlockSpec((tf, D), lambda i, j: (j, 0)),
        ],
        out_specs=pl.BlockSpec((tm, D), lambda i, j: (i, 0)),
        out_shape=jax.ShapeDtypeStruct((M, D), F32),
        scratch_shapes=[pltpu.VMEM((tm, D), BF16)],
        compiler_params=_cparams(("parallel", "arbitrary")),
        name="swiglu_ffn",
    )(x2, g, scale, shift, gate, w1, w3, w2)


def _gelu(x):
    return 0.5 * x * (1.0 + lax.erf(x * np.float32(1.0 / np.sqrt(2.0))))


def _sgu_kernel(u_ref, v_ref, g_ref, b_ref, ws_ref, bias_ref, o_ref, *, chunks):
    m0 = _half_masks((BLOCK, 128))
    for c in range(chunks):
        rows = slice(c * BLOCK, (c + 1) * BLOCK)
        u = _gelu(u_ref[rows, :])
        v = _gelu(v_ref[rows, :])
        mu = jnp.mean(v, axis=-1, keepdims=True)
        vc = v - mu
        var = jnp.mean(vc * vc, axis=-1, keepdims=True)
        vn = vc * lax.rsqrt(var + LN_EPS) * g_ref[...] + b_ref[...]
        outs = []
        for j in range(GH // 2):
            vp = vn[:, 128 * j:128 * (j + 1)]
            stacked = jnp.concatenate(
                [jnp.where(m0, vp, 0.0), jnp.where(m0, 0.0, vp)], axis=0).astype(BF16)
            outs.append(_dot(ws_ref[j], stacked))
        z = jnp.concatenate(outs, axis=1) + bias_ref[...]
        o_ref[rows, :] = (u * z).astype(BF16)


def _sgu_call(proj, ln_g, ln_b, ws_cat, bias_full):
    M = proj.shape[0]
    chunks = 4
    tr = chunks * BLOCK
    return pl.pallas_call(
        functools.partial(_sgu_kernel, chunks=chunks),
        grid=(M // tr,),
        in_specs=[
            pl.BlockSpec((tr, GW), lambda i: (i, OFF_SU // GW)),
            pl.BlockSpec((tr, GW), lambda i: (i, OFF_SV // GW)),
            pl.BlockSpec((1, GW), lambda i: (0, 0)),
            pl.BlockSpec((1, GW), lambda i: (0, 0)),
            pl.BlockSpec((GH // 2, BLOCK, 2 * BLOCK), lambda i: (0, 0, 0)),
            pl.BlockSpec((BLOCK, GW), lambda i: (0, 0)),
        ],
        out_specs=pl.BlockSpec((tr, GW), lambda i: (i, 0)),
        out_shape=jax.ShapeDtypeStruct((M, GW), BF16),
        compiler_params=_cparams(("parallel",)),
        name="sgu_mix",
    )(proj, proj, ln_g, ln_b, ws_cat, bias_full)


def _swa_kernel(sink_ref, q_ref, kv_ref, kvp_ref, qg_ref, kg_ref, e_ref, o_ref, *, blocks, blocks_per_seq):
    i = pl.program_id(0)
    m0q = _half_masks((BLOCK, 128))
    m0k = _half_masks((2 * BLOCK, 128))
    row = lax.broadcasted_iota(jnp.int32, (BLOCK, 2 * BLOCK), 0)
    col = lax.broadcasted_iota(jnp.int32, (BLOCK, 2 * BLOCK), 1)
    band = (col <= row + BLOCK) & (col > row)
    e128 = e_ref.at[0:128, 0:128]

    kv_all = jnp.concatenate([kvp_ref[...], kv_ref[...]], axis=0)
    for r in range(blocks):
        first = ((i * blocks + r) % blocks_per_seq) == 0
        mask = band & (col >= jnp.where(first, BLOCK, 0))
        q = q_ref[r * BLOCK:(r + 1) * BLOCK, :]
        qn = q * lax.rsqrt(_seg_mean(q * q, e_ref) + NORM_EPS) * qg_ref[...]
        kv = kv_all[r * BLOCK:(r + 2) * BLOCK, :]
        k = kv[:, 0:128]
        v = kv[:, 128:256]
        kn = k * lax.rsqrt(_seg_mean(k * k, e128) + NORM_EPS) * kg_ref[...]
        kn_sw = pltpu.roll(kn, 64, axis=1)
        v_sw = pltpu.roll(v, 64, axis=1)
        kdup = [jnp.where(m0k, kn, kn_sw).astype(BF16), jnp.where(m0k, kn_sw, kn).astype(BF16)]
        vdup = [jnp.where(m0k, v, v_sw), jnp.where(m0k, v_sw, v)]
        vmsk = [[jnp.where(m0k, vd, 0.0).astype(BF16), jnp.where(m0k, 0.0, vd).astype(BF16)] for vd in vdup]
        scores = []
        for h in range(GH):
            qp = qn[:, 128 * (h // 2):128 * (h // 2 + 1)]
            qm = (jnp.where(m0q, qp, 0.0) if h % 2 == 0 else jnp.where(m0q, 0.0, qp)).astype(BF16)
            scores.append(_dot_nt(qm, kdup[h // SWA_GROUPS]))
        probs = []
        for h in range(GH):
            s = jnp.where(mask, scores[h] * np.float32(HEAD_DIM ** -0.5), NEG_INF)
            sink = sink_ref[h]
            m = jnp.maximum(jnp.max(s, axis=-1, keepdims=True), sink)
            p = jnp.exp(s - m)
            den = jnp.sum(p, axis=-1, keepdims=True) + jnp.exp(sink - m)
            probs.append((p / den).astype(BF16))
        outs = []
        for j in range(GH // 2):
            g = (2 * j) // SWA_GROUPS
            outs.append(_dot(probs[2 * j], vmsk[g][0]) + _dot(probs[2 * j + 1], vmsk[g][1]))
        o_ref[r * BLOCK:(r + 1) * BLOCK, :] = jnp.concatenate(outs, axis=1).astype(BF16)


def _swa_call(proj, sinks, qg, kg, e512, T):
    M = proj.shape[0]
    blocks = 2
    tr = blocks * BLOCK
    return pl.pallas_call(
        functools.partial(_swa_kernel, blocks=blocks, blocks_per_seq=T // BLOCK),
        grid=(M // tr,),
        in_specs=[
            pl.BlockSpec(memory_space=pltpu.SMEM),
            pl.BlockSpec((tr, GW), lambda i: (i, OFF_WQ // GW)),
            pl.BlockSpec((tr, 256), lambda i: (i, OFF_WKV // 256)),
            pl.BlockSpec((BLOCK, 256), lambda i: (jnp.maximum(i * blocks - 1, 0), OFF_WKV // 256)),
            pl.BlockSpec((1, GW), lambda i: (0, 0)),
            pl.BlockSpec((1, 128), lambda i: (0, 0)),
            pl.BlockSpec((GW, GW), lambda i: (0, 0)),
        ],
        out_specs=pl.BlockSpec((tr, GW), lambda i: (i, 0)),
        out_shape=jax.ShapeDtypeStruct((M, GW), BF16),
        compiler_params=_cparams(("parallel",)),
        name="swa_mix",
    )(sinks, proj, proj, proj, qg, kg, e512)


def _fox_prep_kernel(q_ref, k_ref, v_ref, f_ref, qg_ref, kg_ref, fb_ref, e_ref, tri_ref, sel_ref, ones_ref,
                     qt_ref, ka_ref, vt_ref, carry_ref):
    @pl.when(pl.program_id(1) == 0)
    def _():
        carry_ref[...] = jnp.zeros_like(carry_ref)

    q = q_ref[...]
    k = k_ref[...]
    v = v_ref[...]
    tr = q.shape[0]
    qn = q * lax.rsqrt(_seg_mean(q * q, e_ref) + NORM_EPS) * qg_ref[...] * np.float32(HEAD_DIM ** -0.5 * LOG2E)
    kn = k * lax.rsqrt(_seg_mean(k * k, e_ref) + NORM_EPS) * kg_ref[...]
    z = f_ref[...] + fb_ref[...]
    logf2 = (jnp.minimum(z, 0.0) - jnp.log1p(jnp.exp(-jnp.abs(z)))) * np.float32(LOG2E)
    cum = jnp.dot(tri_ref[...], logf2, preferred_element_type=F32,
                  precision=lax.Precision.HIGHEST) + carry_ref[...]
    carry_ref[...] = cum[tr - 1:tr, :]
    neg = -cum
    hi = neg.astype(BF16)
    r1 = neg - hi.astype(F32)
    mid = r1.astype(BF16)
    lo = (r1 - mid.astype(F32)).astype(BF16)
    bias = _dot(jnp.concatenate([hi, mid, lo], axis=1), sel_ref[...])
    m0 = _half_masks((tr, 128))
    for h in range(GH):
        p, half = divmod(h, 2)
        slot = slice(128 * h, 128 * (h + 1))
        pair = slice(128 * p, 128 * (p + 1))
        own_k, own_q, other_k, other_q = kn[:, pair], qn[:, pair], bias[:, slot], ones_ref[:, slot]
        if half == 0:
            ka = jnp.where(m0, own_k, other_k)
            qa = jnp.where(m0, own_q, other_q)
        else:
            ka = jnp.where(m0, other_k, own_k)
            qa = jnp.where(m0, other_q, own_q)
        ka_ref[:, slot] = ka.astype(BF16)
        qt_ref[0, slot, :] = qa.T.astype(BF16)
    for p in range(GH // 2):
        pair = slice(128 * p, 128 * (p + 1))
        vt_ref[0, pair, :] = v[:, pair].T.astype(BF16)


def _fox_prep_call(proj, qg, kg, fb, e512, T):
    M = proj.shape[0]
    B = M // T
    tr = 256
    nb = T // tr
    tri = jnp.tril(jnp.ones((tr, tr), F32))
    sel = np.zeros((3 * 128, GH * 128), np.float32)
    ones = np.zeros((1, GH * 128), np.float32)
    for h in range(GH):
        off = 128 * h + (64 if h % 2 == 0 else 0)
        for part in range(3):
            sel[128 * part + h, off + part] = 1.0
            ones[0, off + part] = 1.0
    cspec = lambda c: pl.BlockSpec((tr, GW), lambda b, i, c=c: (b * nb + i, c))
    return pl.pallas_call(
        _fox_prep_kernel,
        grid=(B, nb),
        in_specs=[
            cspec(OFF_FQ // GW), cspec(OFF_FK // GW), cspec(OFF_FV // GW),
            pl.BlockSpec((tr, 128), lambda b, i: (b * nb + i, OFF_FF // 128)),
            pl.BlockSpec((1, GW), lambda b, i: (0, 0)),
            pl.BlockSpec((1, GW), lambda b, i: (0, 0)),
            pl.BlockSpec((1, 128), lambda b, i: (0, 0)),
            pl.BlockSpec((GW, GW), lambda b, i: (0, 0)),
            pl.BlockSpec((tr, tr), lambda b, i: (0, 0)),
            pl.BlockSpec((3 * 128, GH * 128), lambda b, i: (0, 0)),
            pl.BlockSpec((1, GH * 128), lambda b, i: (0, 0)),
        ],
        out_specs=[
            pl.BlockSpec((1, GH * 128, tr), lambda b, i: (b, 0, i)),
            pl.BlockSpec((tr, GH * 128), lambda b, i: (b * nb + i, 0)),
            pl.BlockSpec((1, GW, tr), lambda b, i: (b, 0, i)),
        ],
        out_shape=[
            jax.ShapeDtypeStruct((B, GH * 128, T), BF16),
            jax.ShapeDtypeStruct((M, GH * 128), BF16),
            jax.ShapeDtypeStruct((B, GW, T), BF16),
        ],
        scratch_shapes=[pltpu.VMEM((1, 128), F32)],
        compiler_params=_cparams(("parallel", "arbitrary")),
        name="fox_prep",
    )(proj, proj, proj, proj, qg, kg, fb, e512, tri, jnp.asarray(sel, BF16), jnp.asarray(ones, F32))


def _fox_kernel(qt_ref, ka_ref, vt_ref, gl_ref, o_ref, m_ref, l_ref, acc_ref, *, tq):
    i = pl.program_id(1)
    m_ref[...] = jnp.full(m_ref.shape, NEG_INF, F32)
    l_ref[...] = jnp.zeros(l_ref.shape, F32)
    acc_ref[...] = jnp.zeros(acc_ref.shape, F32)

    def step(start, nk, masked):
        start = pl.multiple_of(start, tq)
        if masked:
            key = lax.broadcasted_iota(jnp.int32, (nk, tq), 0) + start
            qry = lax.broadcasted_iota(jnp.int32, (nk, tq), 1) + i * tq
            keep = key <= qry
        sts = []
        for h in range(GH):
            slot = slice(128 * h, 128 * (h + 1))
            sts.append(_dot(ka_ref[pl.ds(start, nk), slot], qt_ref[0, slot, :]))
        pts, alphas = [], []
        for h in range(GH):
            st = jnp.where(keep, sts[h], NEG_INF) if masked else sts[h]
            m_old = m_ref[h]
            m_new = jnp.maximum(m_old, jnp.max(st, axis=0, keepdims=True))
            alpha = jnp.exp2(m_old - m_new)
            pt = jnp.exp2(st - m_new)
            l_ref[h] = alpha * l_ref[h] + jnp.sum(pt, axis=0, keepdims=True)
            m_ref[h] = m_new
            pts.append(pt.astype(BF16))
            alphas.append(alpha)
        for h in range(GH):
            vth = vt_ref[0, 64 * h:64 * (h + 1), pl.ds(start, nk)]
            acc_ref[h] = alphas[h] * acc_ref[h] + _dot(vth, pts[h])

    def body(kb, carry):
        step(kb * (2 * tq), 2 * tq, False)
        return carry

    lax.fori_loop(0, i // 2, body, 0)

    @pl.when(i % 2 == 1)
    def _():
        step((i - 1) * tq, 2 * tq, True)

    @pl.when(i % 2 == 0)
    def _():
        step(i * tq, tq, True)

    for p in range(GH // 2):
        pair = slice(128 * p, 128 * (p + 1))
        ot = jnp.concatenate([acc_ref[2 * p] / l_ref[2 * p], acc_ref[2 * p + 1] / l_ref[2 * p + 1]], axis=0)
        o_ref[:, pair] = (ot.T * jax.nn.sigmoid(gl_ref[:, pair])).astype(BF16)


def _fox_call(proj, qt, ka, vt, T):
    M = proj.shape[0]
    B = M // T
    tq = 256
    nq = T // tq
    return pl.pallas_call(
        functools.partial(_fox_kernel, tq=tq),
        grid=(B, nq),
        in_specs=[
            pl.BlockSpec((1, GH * 128, tq), lambda b, i: (b, 0, i)),
            pl.BlockSpec((T, GH * 128), lambda b, i: (b, 0)),
            pl.BlockSpec((1, GW, T), lambda b, i: (b, 0, 0)),
            pl.BlockSpec((tq, GW), lambda b, i: (b * nq + i, OFF_FG // GW)),
        ],
        out_specs=pl.BlockSpec((tq, GW), lambda b, i: (b * nq + i, 0)),
        out_shape=jax.ShapeDtypeStruct((M, GW), BF16),
        scratch_shapes=[
            pltpu.VMEM((GH, 1, tq), F32),
            pltpu.VMEM((GH, 1, tq), F32),
            pltpu.VMEM((GH, HEAD_DIM, tq), F32),
        ],
        compiler_params=_cparams(("parallel", "arbitrary")),
        name="fox_attn",
    )(qt, ka, vt, proj)


def _rwkv_local_kernel(p_ref, lr_ref, pp_ref, lrp_ref, mu_ref, mulr_ref, wlr_ref, w0_ref, a0_ref,
                       kk_ref, ka_ref, rk_ref, e_ref, tri_ref,
                       m_ref, n_ref, r_ref, y0_ref, bonus_ref, g_ref, *, chunks_per_seq, nc):
    C = RWKV_CHUNK
    i = pl.program_id(0)
    has_prev = (((i * nc) % chunks_per_seq) != 0).astype(F32)
    gc = RWKV_CHUNKS_PER_GROUP
    for g in range(nc // gc):
        _rwkv_local_group(g * gc, gc, has_prev, p_ref, lr_ref, pp_ref, lrp_ref, mu_ref, mulr_ref, wlr_ref,
                          w0_ref, a0_ref, kk_ref, ka_ref, rk_ref, e_ref, tri_ref,
                          m_ref, n_ref, r_ref, y0_ref, bonus_ref, g_ref)


def _rwkv_local_group(c0, nc, has_prev, p_ref, lr_ref, pp_ref, lrp_ref, mu_ref, mulr_ref, wlr_ref,
                      w0_ref, a0_ref, kk_ref, ka_ref, rk_ref, e_ref, tri_ref,
                      m_ref, n_ref, r_ref, y0_ref, bonus_ref, g_ref):
    C = RWKV_CHUNK
    r0 = c0 * C
    grows = slice(r0, r0 + nc * C)
    row = lax.broadcasted_iota(jnp.int32, (nc * C, 1), 0)

    def shifted(cur_ref, prev_ref, mu):
        cur = cur_ref[grows, :]
        prev_row = prev_ref[7:8, :] * has_prev if r0 == 0 else cur_ref[r0 - 1:r0, :]
        prev = jnp.where(row == 0, prev_row, pltpu.roll(cur, 1, axis=0))
        return cur + (prev - cur) * mu

    p = shifted(p_ref, pp_ref, mu_ref[...])
    lr = shifted(lr_ref, lrp_ref, mulr_ref[...])
    r = p[:, 0:GW]
    k = p[:, GW:2 * GW]
    v = p[:, 2 * GW:3 * GW]

    lane = lax.broadcasted_iota(jnp.int32, lr.shape, 1)
    z = jnp.where(lane < 32, jnp.tanh(lr), jnp.where(lane < 64, lr, jax.nn.sigmoid(lr)))
    low = _dot(z.astype(BF16), wlr_ref[...])
    x = -(w0_ref[...] + low[:, 0:GW])
    softplus = jnp.maximum(x, 0.0) + jnp.log1p(jnp.exp(-jnp.abs(x)))
    w = -softplus - 0.5
    lw = -jnp.exp(w)
    a = jax.nn.sigmoid(a0_ref[...] + low[:, GW:2 * GW])
    g_ref[grows, :] = low[:, 2 * GW:3 * GW]

    kk = k * kk_ref[...]
    n2 = _seg_mean(kk * kk, e_ref) * np.float32(HEAD_DIM)
    kk = kk / jnp.maximum(jnp.sqrt(n2), 1e-12)
    k = k * (1.0 + (a - 1.0) * ka_ref[...])
    b = kk * a
    bonus_ref[grows, :] = _seg_mean(r * k * rk_ref[...], e_ref) * np.float32(HEAD_DIM) * v

    tri = tri_ref[...]
    lw_hi = lw.astype(BF16)
    lw_lo = (lw - lw_hi.astype(F32)).astype(BF16)
    L_all = _dot(tri, lw_hi) + _dot(tri, lw_lo)

    m0 = _half_masks((C, 128))
    ri = lax.broadcasted_iota(jnp.int32, (2 * C, 2 * C), 0)
    ci = lax.broadcasted_iota(jnp.int32, (2 * C, 2 * C), 1)
    strict = (ri % C) > (ci % C)
    incl = (ri % C) >= (ci % C)
    eye = ri == ci

    def stack(xp):
        return jnp.concatenate([jnp.where(m0, xp, 0.0), jnp.where(m0, 0.0, xp)], axis=0)

    chains = [(c, j) for c in range(nc) for j in range(GH // 2)]
    st = []
    for c, j in chains:
        rs = slice(c * C, (c + 1) * C)
        sl = slice(128 * j, 128 * (j + 1))
        L = L_all[rs, sl]
        Lend = L[C - 1:C, :]
        e_pos = jnp.exp(L)
        e_neg = jnp.exp(-L)
        e_hat = jnp.exp(Lend - L)
        rc, kc, bc = r[rs, sl], k[rs, sl], b[rs, sl]
        rst = stack(rc * e_pos)
        ast_b = stack(-kk[rs, sl] * jnp.exp(L - lw[rs, sl])).astype(BF16)
        st.append(dict(
            rs=rs, sl=sl, rst=rst, ast_b=ast_b, vst_b=stack(v[rs, sl]).astype(BF16),
            left=jnp.concatenate([ast_b, rst.astype(BF16)], axis=0),
            right=jnp.concatenate([stack(bc * e_neg), stack(kc * e_neg)], axis=0).astype(BF16),
            bkh=jnp.concatenate([stack(bc * e_hat), stack(kc * e_hat)], axis=0).astype(BF16),
            wc=jnp.exp(Lend)))
    for s in st:
        G = _dot_nt(s["left"], s["right"])
        a_ab = jnp.where(strict, G[0:2 * C, 0:2 * C], 0.0)
        s["a_ak"] = jnp.where(strict, G[0:2 * C, 2 * C:4 * C], 0.0).astype(BF16)
        s["a_r"] = jnp.concatenate([jnp.where(incl, G[2 * C:4 * C, 0:2 * C], 0.0),
                                    jnp.where(incl, G[2 * C:4 * C, 2 * C:4 * C], 0.0)], axis=1).astype(BF16)
        s["t_inv"] = jnp.where(eye, 1.0, a_ab)
        s["apow"] = a_ab.astype(BF16)
    for s in st:
        s["akv"] = _dot(s["a_ak"], s["vst_b"]).astype(BF16)
    for s in st:
        s["apow"] = _dot(s["apow"], s["apow"]).astype(BF16)
    for _ in range(4):
        for s in st:
            res = _dot(s["apow"], jnp.concatenate([s["apow"], s["t_inv"].astype(BF16)], axis=1))
            s["apow"] = res[:, 0:2 * C].astype(BF16)
            s["t_inv"] = s["t_inv"] + res[:, 2 * C:4 * C]
    for s in st:
        s["t_inv"] = s["t_inv"] + _dot(s["apow"], s["t_inv"].astype(BF16))
    for s in st:
        pq = _dot(s["t_inv"].astype(BF16), jnp.concatenate([s["ast_b"], s["akv"]], axis=1))
        s["rhs2"] = jnp.concatenate(
            [pq.astype(BF16), jnp.concatenate([jnp.zeros_like(s["vst_b"]), s["vst_b"]], axis=1)], axis=0)
    for s in st:
        ry = _dot(s["a_r"], s["rhs2"])
        rhat = s["rst"] + ry[:, 0:128]
        y0 = ry[:, 128:256]
        out_rows = slice(r0 + s["rs"].start, r0 + s["rs"].stop)
        r_ref[out_rows, s["sl"]] = (rhat[0:C, :] + rhat[C:2 * C, :]).astype(BF16)
        y0_ref[out_rows, s["sl"]] = y0[0:C, :] + y0[C:2 * C, :]
    for (c, j), s in zip(chains, st):
        mn = _dot_tn(s["bkh"], s["rhs2"])
        m_ref[c0 + c, j] = (mn[:, 0:128] + jnp.where(eye, s["wc"], 0.0)).astype(BF16)
        n_ref[c0 + c, j] = mn[:, 128:256]


def _rwkv_local_call(proj, mu_rkv, mu_lr, wlr, w0, a0, kk, ka, rk, e512, T):
    M = proj.shape[0]
    C = RWKV_CHUNK
    nc = RWKV_CHUNKS_PER_STEP
    R = nc * C
    GR = RWKV_CHUNKS_PER_GROUP * C
    nchunks = M // C
    ridx = jnp.arange(GR)
    tri = ((ridx[:, None] >= ridx[None, :]) & (ridx[:, None] // C == ridx[None, :] // C)).astype(BF16)
    vec = lambda n: pl.BlockSpec((1, n), lambda i: (0, 0))
    rowspec = pl.BlockSpec((R, GW), lambda i: (i, 0))
    matspec = pl.BlockSpec((nc, GH // 2, 128, 128), lambda i: (i, 0, 0, 0))
    prev = lambda i: jnp.maximum(i * (R // 8) - 1, 0)
    return pl.pallas_call(
        functools.partial(_rwkv_local_kernel, chunks_per_seq=T // C, nc=nc),
        grid=(nchunks // nc,),
        in_specs=[
            pl.BlockSpec((R, 3 * GW), lambda i: (i, OFF_RR // (3 * GW))),
            pl.BlockSpec((R, 256), lambda i: (i, OFF_RLR // 256)),
            pl.BlockSpec((8, 3 * GW), lambda i: (prev(i), OFF_RR // (3 * GW))),
            pl.BlockSpec((8, 256), lambda i: (prev(i), OFF_RLR // 256)),
            vec(3 * GW), vec(256),
            pl.BlockSpec((256, 3 * GW), lambda i: (0, 0)),
            vec(GW), vec(GW), vec(GW), vec(GW), vec(GW),
            pl.BlockSpec((GW, GW), lambda i: (0, 0)),
            pl.BlockSpec((GR, GR), lambda i: (0, 0)),
        ],
        out_specs=[matspec, matspec, rowspec, rowspec, rowspec, rowspec],
        out_shape=[
            jax.ShapeDtypeStruct((nchunks, GH // 2, 128, 128), BF16),
            jax.ShapeDtypeStruct((nchunks, GH // 2, 128, 128), F32),
            jax.ShapeDtypeStruct((M, GW), BF16),
            jax.ShapeDtypeStruct((M, GW), F32),
            jax.ShapeDtypeStruct((M, GW), F32),
            jax.ShapeDtypeStruct((M, GW), F32),
        ],
        compiler_params=_cparams(("parallel",)),
        name="rwkv_chunk_local",
    )(proj, proj, proj, proj, mu_rkv, mu_lr, wlr, w0, a0, kk, ka, rk, e512, tri)


def _rwkv_state_kernel(m_ref, n_ref, r_ref, y0_ref, bonus_ref, g_ref, lg_ref, lb_ref, e_ref,
                       o_ref, h_ref, *, batch, steps):
    @pl.when(pl.program_id(0) == 0)
    def _():
        h_ref[...] = jnp.zeros_like(h_ref)

    C = RWKV_CHUNK
    for c in range(steps):
        rows = slice(c * C, (c + 1) * C)
        ys = {}
        for b in range(batch):
            for j in range(GH // 2):
                sl = slice(128 * j, 128 * (j + 1))
                h = h_ref[b, j].astype(BF16)
                ys[b, j] = _dot(r_ref[b, rows, sl], h) + y0_ref[b, rows, sl]
                h_ref[b, j] = _dot(m_ref[b, c, j], h) + n_ref[b, c, j]
        for b in range(batch):
            y = jnp.concatenate([ys[b, j] for j in range(GH // 2)], axis=1)
            mean = _seg_mean(y, e_ref)
            yc = y - mean
            var = _seg_mean(yc * yc, e_ref)
            yn = yc * lax.rsqrt(var + RWKV_GN_EPS) * lg_ref[...] + lb_ref[...]
            o_ref[b, rows, :] = ((yn + bonus_ref[b, rows, :]) * g_ref[b, rows, :]).astype(BF16)


def _rwkv_state_call(mm, nn, rr, y0, bonus, g, lnx_g, lnx_b, e512, B, T):
    C = RWKV_CHUNK
    steps = RWKV_STATE_CHUNKS_PER_STEP
    nc = T // C
    mm = mm.reshape(B, nc, GH // 2, 128, 128)
    nn = nn.reshape(B, nc, GH // 2, 128, 128)
    r3 = lambda a: a.reshape(B, T, GW)
    matspec = pl.BlockSpec((B, steps, GH // 2, 128, 128), lambda c: (0, c, 0, 0, 0))
    rowspec = pl.BlockSpec((B, steps * C, GW), lambda c: (0, c, 0))
    vec = pl.BlockSpec((1, GW), lambda c: (0, 0))
    out = pl.pallas_call(
        functools.partial(_rwkv_state_kernel, batch=B, steps=steps),
        grid=(nc // steps,),
        in_specs=[matspec, matspec, rowspec, rowspec, rowspec, rowspec, vec, vec,
                  pl.BlockSpec((GW, GW), lambda c: (0, 0))],
        out_specs=rowspec,
        out_shape=jax.ShapeDtypeStruct((B, T, GW), BF16),
        scratch_shapes=[pltpu.VMEM((B, GH // 2, 128, 128), F32)],
        compiler_params=_cparams(("arbitrary",)),
        name="rwkv_state_scan",
    )(mm, nn, r3(rr), r3(y0), r3(bonus), r3(g), lnx_g, lnx_b, e512)
    return out.reshape(B * T, GW)


def _pad_cols(a, n):
    return jnp.pad(a, [(0, 0)] * (a.ndim - 1) + [(0, n - a.shape[-1])])


def _permute_in_cols(w):
    rw, sw, sg, fx = 0, 1696, 2464, 3488
    parts = [
        w[..., fx:fx + 1536],
        w[..., fx + 1544:fx + 2056],
        w[..., sg:sg + 1024],
        w[..., rw:rw + 1536],
        w[..., sw:sw + 768],
        _pad_cols(w[..., rw + 1536:rw + 1696], 256),
        _pad_cols(w[..., fx + 1536:fx + 1544], 128),
    ]
    return jnp.concatenate(parts, axis=-1)


def _tile_heads(g):
    return jnp.tile(g, GH).reshape(1, GW)


def kernel(x, c, w_mod, b_mod, norm1_g, norm2_g, w_in, w_out, rwkv_mu, rwkv_w0, rwkv_w2, rwkv_a0, rwkv_a2, rwkv_g2, rwkv_kk, rwkv_ka, rwkv_rk, rwkv_lnx_g, rwkv_lnx_b, swa_qn_g, swa_kn_g, swa_sinks, sgu_ln_g, sgu_ln_b, sgu_ws, sgu_b, fox_qn_g, fox_kn_g, fox_fb, ffn_w1, ffn_w3, ffn_w2):
    B, T, D = x.shape
    L = w_mod.shape[0]
    M = B * T

    mod = _mod_call(c, w_mod, b_mod)
    mod = mod.reshape(L, B, 6, 1, D)

    seg = jnp.arange(GW) // HEAD_DIM
    e512 = jnp.where(seg[:, None] == seg[None, :], 1.0 / HEAD_DIM, 0.0).astype(BF16)
    rows, cols = np.tril_indices(BLOCK)

    x2 = x.reshape(M, D)
    for l in range(L):
        shift1, scale1, gate1, shift2, scale2, gate2 = [mod[l, :, s] for s in range(6)]
        w_in_l = _permute_in_cols(w_in[l]).astype(BF16)
        proj = _in_call(x2, norm1_g[l].reshape(1, D), scale1, shift1, w_in_l, T)

        mu = rwkv_mu[l]
        mu_rkv = mu[0:3 * GW].reshape(1, 3 * GW)
        mu_lr = _pad_cols(mu[3 * GW:], 256).reshape(1, 256)
        wlr = jnp.zeros((256, 3 * GW), F32)
        wlr = wlr.at[0:32, 0:GW].set(rwkv_w2[l])
        wlr = wlr.at[32:64, GW:2 * GW].set(rwkv_a2[l])
        wlr = wlr.at[64:160, 2 * GW:3 * GW].set(rwkv_g2[l])
        vec = lambda a: a.reshape(1, GW)
        mm, nn, rr, y0, bonus, g = _rwkv_local_call(
            proj, mu_rkv, mu_lr, wlr.astype(BF16), vec(rwkv_w0[l]), vec(rwkv_a0[l]), vec(rwkv_kk[l]),
            vec(rwkv_ka[l]), vec(rwkv_rk[l]), e512, T)
        ya = _rwkv_state_call(mm, nn, rr, y0, bonus, g, vec(rwkv_lnx_g[l]), vec(rwkv_lnx_b[l]), e512, B, T)

        yb = _swa_call(proj, swa_sinks[l], _tile_heads(swa_qn_g[l]), jnp.tile(swa_kn_g[l], 2).reshape(1, 128),
                       e512, T)

        w_s = jnp.zeros((GH, BLOCK, BLOCK), F32).at[:, rows, cols].set(sgu_ws[l])
        ws_cat = jnp.concatenate([w_s[0::2], w_s[1::2]], axis=2).astype(BF16)
        bias_full = jnp.repeat(jnp.transpose(sgu_b[l]), HEAD_DIM, axis=1)
        yc = _sgu_call(proj, vec(sgu_ln_g[l]), vec(sgu_ln_b[l]), ws_cat, bias_full)

        fb = _pad_cols(fox_fb[l], 128).reshape(1, 128)
        qt, ka, vt = _fox_prep_call(proj, _tile_heads(fox_qn_g[l]), _tile_heads(fox_kn_g[l]), fb, e512, T)
        yd = _fox_call(proj, qt, ka, vt, T)

        x2 = _out_call(ya, yb, yc, yd, w_out[l].astype(BF16), x2, gate1, T)
        x2 = _ffn_call(x2, norm2_g[l].reshape(1, D), scale2, shift2, gate2,
                       ffn_w1[l].astype(BF16), ffn_w3[l].astype(BF16), ffn_w2[l].astype(BF16), T)
    return x2.reshape(B, T, D)
```

```python
import functools

import numpy as np
import jax
import jax.numpy as jnp
from jax import lax
from jax.experimental import pallas as pl
from jax.experimental.pallas import tpu as pltpu

F32 = jnp.float32
BF16 = jnp.bfloat16

D_MODEL = 2048
HEAD_DIM = 64
GW = 512
GH = 8
BLOCK = 128
NORM_EPS = 1e-6
LN_EPS = 1e-5
NEG_INF = -1e30
RWKV_GN_EPS = 64e-5
NORM_ROWS = 16
SWA_GROUPS = 4
FOX_SKEW = (3, 6)
RWKV_CHUNK = 64
RWKV_CHUNKS_PER_STEP = 4
RWKV_STATE_CHUNKS_PER_STEP = 2
LOG2E = float(np.log2(np.e))
FFN_HIDDEN = 5632

OFF_FQ, OFF_FK, OFF_FV, OFF_FG = 0, 512, 1024, 1536
OFF_SU, OFF_SV = 2048, 2560
OFF_RR = 3072
OFF_WQ = 4608
OFF_WKV = 5120
OFF_RLR = 5376
OFF_FF = 5632
NP = 5760

VMEM_LIMIT = 56 * 1024 * 1024


def _cparams(sem):
    return pltpu.CompilerParams(dimension_semantics=sem, vmem_limit_bytes=VMEM_LIMIT)


def _dot(a, b):
    return jnp.dot(a, b, preferred_element_type=F32)


def _dot_nt(a, b):
    return lax.dot_general(a, b, (((1,), (1,)), ((), ())), preferred_element_type=F32)


def _dot_tn(a, b):
    return lax.dot_general(a, b, (((0,), (0,)), ((), ())), preferred_element_type=F32)


def _seg_mean(x, e_ref):
    return _dot(x.astype(BF16), e_ref[...])


def _half_masks(shape):
    lane = lax.broadcasted_iota(jnp.int32, shape, len(shape) - 1)
    m0 = (lane % 128) < 64
    return m0


def _mod_kernel(c_ref, w_ref, b_ref, o_ref):
    c = c_ref[...]
    ca = (c * jax.nn.sigmoid(c)).astype(BF16)
    o_ref[0] = _dot(ca, w_ref[0].astype(BF16)) + b_ref[0]


def _mod_call(c, w_mod, b_mod):
    L, D, N = w_mod.shape
    B = c.shape[0]
    tn = 1024
    return pl.pallas_call(
        _mod_kernel,
        grid=(L, N // tn),
        in_specs=[
            pl.BlockSpec((B, D), lambda l, j: (0, 0)),
            pl.BlockSpec((1, D, tn), lambda l, j: (l, 0, j)),
            pl.BlockSpec((1, 1, tn), lambda l, j: (l, 0, j)),
        ],
        out_specs=pl.BlockSpec((1, B, tn), lambda l, j: (l, 0, j)),
        out_shape=jax.ShapeDtypeStruct((L, B, N), F32),
        compiler_params=_cparams(("parallel", "parallel")),
        name="adaln_mod",
    )(c, w_mod, b_mod.reshape(L, 1, N))


def _norm_modulate(x_ref, g_ref, sc_ref, sh_ref, h_ref):
    gs = g_ref[...] * (1.0 + sc_ref[0])
    sh = sh_ref[0]

    def body(c, carry):
        rows = pl.ds(pl.multiple_of(c * NORM_ROWS, NORM_ROWS), NORM_ROWS)
        x = x_ref[rows, :]
        ms = jnp.mean(x * x, axis=-1, keepdims=True)
        h_ref[rows, :] = (x * lax.rsqrt(ms + NORM_EPS) * gs + sh).astype(BF16)
        return carry

    lax.fori_loop(0, x_ref.shape[0] // NORM_ROWS, body, 0, unroll=8)


def _in_kernel(x_ref, g_ref, sc_ref, sh_ref, w_ref, o_ref, h_ref):
    @pl.when(pl.program_id(1) == 0)
    def _():
        _norm_modulate(x_ref, g_ref, sc_ref, sh_ref, h_ref)

    o_ref[...] = _dot(h_ref[...], w_ref[...])


def _in_call(x2, g, scale, shift, w, T):
    M, D = x2.shape
    n = w.shape[1]
    tm, tn = 1024, 1920
    return pl.pallas_call(
        _in_kernel,
        grid=(M // tm, n // tn),
        in_specs=[
            pl.BlockSpec((tm, D), lambda i, j: (i, 0)),
            pl.BlockSpec((1, D), lambda i, j: (0, 0)),
            pl.BlockSpec((1, 1, D), lambda i, j: (i * tm // T, 0, 0)),
            pl.BlockSpec((1, 1, D), lambda i, j: (i * tm // T, 0, 0)),
            pl.BlockSpec((D, tn), lambda i, j: (0, j)),
        ],
        out_specs=pl.BlockSpec((tm, tn), lambda i, j: (i, j)),
        out_shape=jax.ShapeDtypeStruct((M, n), F32),
        scratch_shapes=[pltpu.VMEM((tm, D), BF16)],
        compiler_params=_cparams(("parallel", "arbitrary")),
        name="norm_in_proj",
    )(x2, g, scale, shift, w)


def _out_kernel(ya_ref, yb_ref, yc_ref, yd_ref, w_ref, x_ref, gt_ref, o_ref):
    acc = _dot(ya_ref[...], w_ref[0:GW, :])
    acc += _dot(yb_ref[...], w_ref[GW:2 * GW, :])
    acc += _dot(yc_ref[...], w_ref[2 * GW:3 * GW, :])
    acc += _dot(yd_ref[...], w_ref[3 * GW:4 * GW, :])
    o_ref[...] = x_ref[...] + gt_ref[0] * acc


def _out_call(ya, yb, yc, yd, w, x2, gate, T):
    M, D = x2.shape
    tm = 512
    yspec = pl.BlockSpec((tm, GW), lambda i: (i, 0))
    return pl.pallas_call(
        _out_kernel,
        grid=(M // tm,),
        in_specs=[
            yspec, yspec, yspec, yspec,
            pl.BlockSpec((D, D), lambda i: (0, 0)),
            pl.BlockSpec((tm, D), lambda i: (i, 0)),
            pl.BlockSpec((1, 1, D), lambda i: (i * tm // T, 0, 0)),
        ],
        out_specs=pl.BlockSpec((tm, D), lambda i: (i, 0)),
        out_shape=jax.ShapeDtypeStruct((M, D), F32),
        compiler_params=_cparams(("parallel",)),
        name="out_proj_residual",
    )(ya, yb, yc, yd, w, x2, gate)


def _ffn_kernel(x_ref, g_ref, sc_ref, sh_ref, gt_ref, w1_ref, w3_ref, w2_ref, o_ref, h_ref):
    @pl.when(pl.program_id(1) == 0)
    def _():
        _norm_modulate(x_ref, g_ref, sc_ref, sh_ref, h_ref)
        o_ref[...] = x_ref[...]

    h = h_ref[...]
    half = w1_ref.shape[1] // 2
    fs = []
    for c in range(2):
        cols = slice(c * half, (c + 1) * half)
        a = _dot(h, w1_ref[:, cols])
        b = _dot(h, w3_ref[:, cols])
        fs.append((a * jax.nn.sigmoid(a) * b).astype(BF16))
    o_ref[...] += gt_ref[0] * _dot(jnp.concatenate(fs, axis=1), w2_ref[...])


def _ffn_call(x2, g, scale, shift, gate, w1, w3, w2, T):
    M, D = x2.shape
    F = w1.shape[1]
    tm, tf = 1024, 512
    bspec = pl.BlockSpec((1, 1, D), lambda i, j: (i * tm // T, 0, 0))
    return pl.pallas_call(
        _ffn_kernel,
        grid=(M // tm, F // tf),
        in_specs=[
            pl.BlockSpec((tm, D), lambda i, j: (i, 0)),
            pl.BlockSpec((1, D), lambda i, j: (0, 0)),
            bspec, bspec, bspec,
            pl.BlockSpec((D, tf), lambda i, j: (0, j)),
            pl.BlockSpec((D, tf), lambda i, j: (0, j)),
            pl.BlockSpec((tf, D), lambda i, j: (j, 0)),
        ],
        out_specs=pl.BlockSpec((tm, D), lambda i, j: (i, 0)),
        out_shape=jax.ShapeDtypeStruct((M, D), F32),
        scratch_shapes=[pltpu.VMEM((tm, D), BF16)],
        compiler_params=_cparams(("parallel", "arbitrary")),
        name="swiglu_ffn",
    )(x2, g, scale, shift, gate, w1, w3, w2)


def _gelu(x):
    return 0.5 * x * (1.0 + lax.erf(x * np.float32(1.0 / np.sqrt(2.0))))


def _sgu_kernel(u_ref, v_ref, g_ref, b_ref, ws_ref, bias_ref, o_ref, *, chunks):
    m0 = _half_masks((BLOCK, 128))
    for c in range(chunks):
        rows = slice(c * BLOCK, (c + 1) * BLOCK)
        u = _gelu(u_ref[rows, :])
        v = _gelu(v_ref[rows, :])
        mu = jnp.mean(v, axis=-1, keepdims=True)
        vc = v - mu
        var = jnp.mean(vc * vc, axis=-1, keepdims=True)
        vn = vc * lax.rsqrt(var + LN_EPS) * g_ref[...] + b_ref[...]
        outs = []
        for j in range(GH // 2):
            vp = vn[:, 128 * j:128 * (j + 1)]
            stacked = jnp.concatenate(
                [jnp.where(m0, vp, 0.0), jnp.where(m0, 0.0, vp)], axis=0).astype(BF16)
            outs.append(_dot(ws_ref[j], stacked))
        z = jnp.concatenate(outs, axis=1) + bias_ref[...]
        o_ref[rows, :] = (u * z).astype(BF16)


def _sgu_call(proj, ln_g, ln_b, ws_cat, bias_full):
    M = proj.shape[0]
    chunks = 4
    tr = chunks * BLOCK
    return pl.pallas_call(
        functools.partial(_sgu_kernel, chunks=chunks),
        grid=(M // tr,),
        in_specs=[
            pl.BlockSpec((tr, GW), lambda i: (i, OFF_SU // GW)),
            pl.BlockSpec((tr, GW), lambda i: (i, OFF_SV // GW)),
            pl.BlockSpec((1, GW), lambda i: (0, 0)),
            pl.BlockSpec((1, GW), lambda i: (0, 0)),
            pl.BlockSpec((GH // 2, BLOCK, 2 * BLOCK), lambda i: (0, 0, 0)),
            pl.BlockSpec((BLOCK, GW), lambda i: (0, 0)),
        ],
        out_specs=pl.BlockSpec((tr, GW), lambda i: (i, 0)),
        out_shape=jax.ShapeDtypeStruct((M, GW), BF16),
        compiler_params=_cparams(("parallel",)),
        name="sgu_mix",
    )(proj, proj, ln_g, ln_b, ws_cat, bias_full)


def _swa_kernel(sink_ref, q_ref, kv_ref, kvp_ref, qg_ref, kg_ref, e_ref, o_ref, *, blocks, blocks_per_seq):
    i = pl.program_id(0)
    m0q = _half_masks((BLOCK, 128))
    m0k = _half_masks((2 * BLOCK, 128))
    row = lax.broadcasted_iota(jnp.int32, (BLOCK, 2 * BLOCK), 0)
    col = lax.broadcasted_iota(jnp.int32, (BLOCK, 2 * BLOCK), 1)
    band = (col <= row + BLOCK) & (col > row)
    e128 = e_ref.at[0:128, 0:128]

    kv_all = jnp.concatenate([kvp_ref[...], kv_ref[...]], axis=0)
    for r in range(blocks):
        first = ((i * blocks + r) % blocks_per_seq) == 0
        mask = band & (col >= jnp.where(first, BLOCK, 0))
        q = q_ref[r * BLOCK:(r + 1) * BLOCK, :]
        qn = q * lax.rsqrt(_seg_mean(q * q, e_ref) + NORM_EPS) * qg_ref[...] * np.float32(HEAD_DIM ** -0.5 * LOG2E)
        kv = kv_all[r * BLOCK:(r + 2) * BLOCK, :]
        k = kv[:, 0:128]
        v = kv[:, 128:256]
        kn = k * lax.rsqrt(_seg_mean(k * k, e128) + NORM_EPS) * kg_ref[...]
        kn_sw = pltpu.roll(kn, 64, axis=1)
        v_sw = pltpu.roll(v, 64, axis=1)
        kdup = [jnp.where(m0k, kn, kn_sw).astype(BF16), jnp.where(m0k, kn_sw, kn).astype(BF16)]
        vdup = [jnp.where(m0k, v, v_sw), jnp.where(m0k, v_sw, v)]
        vmsk = [[jnp.where(m0k, vd, 0.0).astype(BF16), jnp.where(m0k, 0.0, vd).astype(BF16)] for vd in vdup]
        scores = []
        for h in range(GH):
            qp = qn[:, 128 * (h // 2):128 * (h // 2 + 1)]
            qm = (jnp.where(m0q, qp, 0.0) if h % 2 == 0 else jnp.where(m0q, 0.0, qp)).astype(BF16)
            scores.append(_dot_nt(qm, kdup[h // SWA_GROUPS]))
        probs, inv = [], []
        for h in range(GH):
            s = jnp.where(mask, scores[h], NEG_INF)
            sink = sink_ref[h] * np.float32(LOG2E)
            m = jnp.maximum(jnp.max(s, axis=-1, keepdims=True), sink)
            p = jnp.exp2(s - m)
            inv.append(1.0 / (jnp.sum(p, axis=-1, keepdims=True) + jnp.exp2(sink - m)))
            probs.append(p.astype(BF16))
        outs = []
        for j in range(GH // 2):
            g = (2 * j) // SWA_GROUPS
            outs.append(_dot(probs[2 * j], vmsk[g][0]) * inv[2 * j]
                        + _dot(probs[2 * j + 1], vmsk[g][1]) * inv[2 * j + 1])
        o_ref[r * BLOCK:(r + 1) * BLOCK, :] = jnp.concatenate(outs, axis=1).astype(BF16)


def _swa_call(proj, sinks, qg, kg, e512, T):
    M = proj.shape[0]
    blocks = 2
    tr = blocks * BLOCK
    return pl.pallas_call(
        functools.partial(_swa_kernel, blocks=blocks, blocks_per_seq=T // BLOCK),
        grid=(M // tr,),
        in_specs=[
            pl.BlockSpec(memory_space=pltpu.SMEM),
            pl.BlockSpec((tr, GW), lambda i: (i, OFF_WQ // GW)),
            pl.BlockSpec((tr, 256), lambda i: (i, OFF_WKV // 256)),
            pl.BlockSpec((BLOCK, 256), lambda i: (jnp.maximum(i * blocks - 1, 0), OFF_WKV // 256)),
            pl.BlockSpec((1, GW), lambda i: (0, 0)),
            pl.BlockSpec((1, 128), lambda i: (0, 0)),
            pl.BlockSpec((GW, GW), lambda i: (0, 0)),
        ],
        out_specs=pl.BlockSpec((tr, GW), lambda i: (i, 0)),
        out_shape=jax.ShapeDtypeStruct((M, GW), BF16),
        compiler_params=_cparams(("parallel",)),
        name="swa_mix",
    )(sinks, proj, proj, proj, qg, kg, e512)


def _fox_prep_kernel(q_ref, k_ref, v_ref, f_ref, qg_ref, kg_ref, fb_ref, e_ref, tri_ref, sel_ref, ones_ref,
                     qt_ref, ka_ref, vt_ref, carry_ref):
    @pl.when(pl.program_id(1) == 0)
    def _():
        carry_ref[...] = jnp.zeros_like(carry_ref)

    q = q_ref[...]
    k = k_ref[...]
    v = v_ref[...]
    tr = q.shape[0]
    qn = q * lax.rsqrt(_seg_mean(q * q, e_ref) + NORM_EPS) * qg_ref[...] * np.float32(HEAD_DIM ** -0.5 * LOG2E)
    kn = k * lax.rsqrt(_seg_mean(k * k, e_ref) + NORM_EPS) * kg_ref[...]
    z = f_ref[...] + fb_ref[...]
    logf2 = (jnp.minimum(z, 0.0) - jnp.log1p(jnp.exp(-jnp.abs(z)))) * np.float32(LOG2E)
    cum = jnp.dot(tri_ref[...], logf2, preferred_element_type=F32,
                  precision=lax.Precision.HIGHEST) + carry_ref[...]
    carry_ref[...] = cum[tr - 1:tr, :]
    neg = -cum
    hi = neg.astype(BF16)
    r1 = neg - hi.astype(F32)
    mid = r1.astype(BF16)
    lo = (r1 - mid.astype(F32)).astype(BF16)
    bias = _dot(jnp.concatenate([hi, mid, lo], axis=1), sel_ref[...])
    m0 = _half_masks((tr, 128))
    for h in range(GH):
        p, half = divmod(h, 2)
        slot = slice(128 * h, 128 * (h + 1))
        pair = slice(128 * p, 128 * (p + 1))
        own_k, own_q, other_k, other_q = kn[:, pair], qn[:, pair], bias[:, slot], ones_ref[:, slot]
        if half == 0:
            ka = jnp.where(m0, own_k, other_k)
            qa = jnp.where(m0, own_q, other_q)
        else:
            ka = jnp.where(m0, other_k, own_k)
            qa = jnp.where(m0, other_q, own_q)
        ka_ref[:, slot] = ka.astype(BF16)
        qt_ref[0, slot, :] = qa.T.astype(BF16)
    for p in range(GH // 2):
        pair = slice(128 * p, 128 * (p + 1))
        vt_ref[0, pair, :] = v[:, pair].T.astype(BF16)


def _fox_prep_call(proj, qg, kg, fb, e512, T):
    M = proj.shape[0]
    B = M // T
    tr = 256
    nb = T // tr
    tri = jnp.tril(jnp.ones((tr, tr), F32))
    sel = np.zeros((3 * 128, GH * 128), np.float32)
    ones = np.zeros((1, GH * 128), np.float32)
    for h in range(GH):
        off = 128 * h + (64 if h % 2 == 0 else 0)
        for part in range(3):
            sel[128 * part + h, off + part] = 1.0
            ones[0, off + part] = 1.0
    cspec = lambda c: pl.BlockSpec((tr, GW), lambda b, i, c=c: (b * nb + i, c))
    return pl.pallas_call(
        _fox_prep_kernel,
        grid=(B, nb),
        in_specs=[
            cspec(OFF_FQ // GW), cspec(OFF_FK // GW), cspec(OFF_FV // GW),
            pl.BlockSpec((tr, 128), lambda b, i: (b * nb + i, OFF_FF // 128)),
            pl.BlockSpec((1, GW), lambda b, i: (0, 0)),
            pl.BlockSpec((1, GW), lambda b, i: (0, 0)),
            pl.BlockSpec((1, 128), lambda b, i: (0, 0)),
            pl.BlockSpec((GW, GW), lambda b, i: (0, 0)),
            pl.BlockSpec((tr, tr), lambda b, i: (0, 0)),
            pl.BlockSpec((3 * 128, GH * 128), lambda b, i: (0, 0)),
            pl.BlockSpec((1, GH * 128), lambda b, i: (0, 0)),
        ],
        out_specs=[
            pl.BlockSpec((1, GH * 128, tr), lambda b, i: (b, 0, i)),
            pl.BlockSpec((tr, GH * 128), lambda b, i: (b * nb + i, 0)),
            pl.BlockSpec((1, GW, tr), lambda b, i: (b, 0, i)),
        ],
        out_shape=[
            jax.ShapeDtypeStruct((B, GH * 128, T), BF16),
            jax.ShapeDtypeStruct((M, GH * 128), BF16),
            jax.ShapeDtypeStruct((B, GW, T), BF16),
        ],
        scratch_shapes=[pltpu.VMEM((1, 128), F32)],
        compiler_params=_cparams(("parallel", "arbitrary")),
        name="fox_prep",
    )(proj, proj, proj, proj, qg, kg, fb, e512, tri, jnp.asarray(sel, BF16), jnp.asarray(ones, F32))


def _fox_kernel(qt_ref, ka_ref, vt_ref, gl_ref, o_ref, m_ref, l_ref, acc_ref, *, tq):
    i = pl.program_id(1)
    m_ref[...] = jnp.full(m_ref.shape, NEG_INF, F32)
    l_ref[...] = jnp.zeros(l_ref.shape, F32)
    acc_ref[...] = jnp.zeros(acc_ref.shape, F32)

    def step(start, nk, masked):
        start = pl.multiple_of(start, tq)
        if masked:
            key = lax.broadcasted_iota(jnp.int32, (nk, tq), 0) + start
            qry = lax.broadcasted_iota(jnp.int32, (nk, tq), 1) + i * tq
            keep = key <= qry
        sts, pts, alphas = {}, {}, {}
        d1, d2 = FOX_SKEW
        for t in range(GH + d2):
            if t < GH:
                slot = slice(128 * t, 128 * (t + 1))
                sts[t] = _dot(ka_ref[pl.ds(start, nk), slot], qt_ref[0, slot, :])
            if d1 <= t < GH + d1:
                h = t - d1
                st = jnp.where(keep, sts.pop(h), NEG_INF) if masked else sts.pop(h)
                m_old = m_ref[h]
                m_new = jnp.maximum(m_old, jnp.max(st, axis=0, keepdims=True))
                alphas[h] = jnp.exp2(m_old - m_new)
                pt = jnp.exp2(st - m_new)
                l_ref[h] = alphas[h] * l_ref[h] + jnp.sum(pt, axis=0, keepdims=True)
                m_ref[h] = m_new
                pts[h] = pt.astype(BF16)
            if t >= d2:
                h = t - d2
                vth = vt_ref[0, 64 * h:64 * (h + 1), pl.ds(start, nk)]
                acc_ref[h] = alphas.pop(h) * acc_ref[h] + _dot(vth, pts.pop(h))

    def body(kb, carry):
        step(kb * (2 * tq), 2 * tq, False)
        return carry

    lax.fori_loop(0, i // 2, body, 0)

    @pl.when(i % 2 == 1)
    def _():
        step((i - 1) * tq, 2 * tq, True)

    @pl.when(i % 2 == 0)
    def _():
        step(i * tq, tq, True)

    for p in range(GH // 2):
        pair = slice(128 * p, 128 * (p + 1))
        ot = jnp.concatenate([acc_ref[2 * p] / l_ref[2 * p], acc_ref[2 * p + 1] / l_ref[2 * p + 1]], axis=0)
        o_ref[:, pair] = (ot.T * jax.nn.sigmoid(gl_ref[:, pair])).astype(BF16)


def _fox_call(proj, qt, ka, vt, T):
    M = proj.shape[0]
    B = M // T
    tq = 256
    nq = T // tq
    return pl.pallas_call(
        functools.partial(_fox_kernel, tq=tq),
        grid=(B, nq),
        in_specs=[
            pl.BlockSpec((1, GH * 128, tq), lambda b, i: (b, 0, i)),
            pl.BlockSpec((T, GH * 128), lambda b, i: (b, 0)),
            pl.BlockSpec((1, GW, T), lambda b, i: (b, 0, 0)),
            pl.BlockSpec((tq, GW), lambda b, i: (b * nq + i, OFF_FG // GW)),
        ],
        out_specs=pl.BlockSpec((tq, GW), lambda b, i: (b * nq + i, 0)),
        out_shape=jax.ShapeDtypeStruct((M, GW), BF16),
        scratch_shapes=[
            pltpu.VMEM((GH, 1, tq), F32),
            pltpu.VMEM((GH, 1, tq), F32),
            pltpu.VMEM((GH, HEAD_DIM, tq), F32),
        ],
        compiler_params=_cparams(("parallel", "arbitrary")),
        name="fox_attn",
    )(qt, ka, vt, proj)


def _rwkv_local_kernel(p_ref, lr_ref, pp_ref, lrp_ref, mu_ref, mulr_ref, wlr_ref, w0_ref, a0_ref,
                       kk_ref, ka_ref, rk_ref, e_ref, tri_ref,
                       m_ref, n_ref, r_ref, y0_ref, bonus_ref, g_ref, *, chunks_per_seq, nc):
    C = RWKV_CHUNK
    i = pl.program_id(0)
    has_prev = (((i * nc) % chunks_per_seq) != 0).astype(F32)
    row = lax.broadcasted_iota(jnp.int32, (nc * C, 1), 0)

    def shifted(cur_ref, prev_ref, mu):
        cur = cur_ref[...]
        prev = jnp.where(row == 0, prev_ref[7:8, :] * has_prev, pltpu.roll(cur, 1, axis=0))
        return cur + (prev - cur) * mu

    p = shifted(p_ref, pp_ref, mu_ref[...])
    lr = shifted(lr_ref, lrp_ref, mulr_ref[...])
    r = p[:, 0:GW]
    k = p[:, GW:2 * GW]
    v = p[:, 2 * GW:3 * GW]

    lane = lax.broadcasted_iota(jnp.int32, lr.shape, 1)
    z = jnp.where(lane < 32, jnp.tanh(lr), jnp.where(lane < 64, lr, jax.nn.sigmoid(lr)))
    low = _dot(z.astype(BF16), wlr_ref[...])
    lw = -np.float32(np.exp(-0.5)) * jax.nn.sigmoid(w0_ref[...] + low[:, 0:GW])
    a = jax.nn.sigmoid(a0_ref[...] + low[:, GW:2 * GW])
    g_ref[...] = low[:, 2 * GW:3 * GW]

    kk = k * kk_ref[...]
    n2 = _seg_mean(kk * kk, e_ref) * np.float32(HEAD_DIM)
    kk = kk * lax.rsqrt(jnp.maximum(n2, 1e-24))
    k = k * (1.0 + (a - 1.0) * ka_ref[...])
    b = kk * a
    bonus_ref[...] = _seg_mean(r * k * rk_ref[...], e_ref) * np.float32(HEAD_DIM) * v

    tri = tri_ref[...]
    lw_hi = lw.astype(BF16)
    lw_lo = (lw - lw_hi.astype(F32)).astype(BF16)
    L_all = _dot(tri, lw_hi) + _dot(tri, lw_lo)

    m0 = _half_masks((C, 128))
    ri = lax.broadcasted_iota(jnp.int32, (2 * C, 2 * C), 0)
    ci = lax.broadcasted_iota(jnp.int32, (2 * C, 2 * C), 1)
    strict = (ri % C) > (ci % C)
    incl = (ri % C) >= (ci % C)
    eye = ri == ci

    def stack(xp):
        return jnp.concatenate([jnp.where(m0, xp, 0.0), jnp.where(m0, 0.0, xp)], axis=0)

    chains = [(c, j) for c in range(nc) for j in range(GH // 2)]
    st = []
    for c, j in chains:
        rs = slice(c * C, (c + 1) * C)
        sl = slice(128 * j, 128 * (j + 1))
        L = L_all[rs, sl]
        Lend = L[C - 1:C, :]
        e_pos = jnp.exp(L)
        e_neg = jnp.exp(-L)
        e_hat = jnp.exp(Lend - L)
        rc, kc, bc = r[rs, sl], k[rs, sl], b[rs, sl]
        rst = stack(rc * e_pos)
        ast_b = stack(-kk[rs, sl] * jnp.exp(L - lw[rs, sl])).astype(BF16)
        st.append(dict(
            rs=rs, sl=sl, rst=rst, ast_b=ast_b, vst_b=stack(v[rs, sl]).astype(BF16),
            left=jnp.concatenate([ast_b, rst.astype(BF16)], axis=0),
            right=jnp.concatenate([stack(bc * e_neg), stack(kc * e_neg)], axis=0).astype(BF16),
            bkh=jnp.concatenate([stack(bc * e_hat), stack(kc * e_hat)], axis=0).astype(BF16),
            wc=jnp.exp(Lend)))
    for s in st:
        G = _dot_nt(s["left"], s["right"])
        a_ab = jnp.where(strict, G[0:2 * C, 0:2 * C], 0.0)
        s["a_ak"] = jnp.where(strict, G[0:2 * C, 2 * C:4 * C], 0.0).astype(BF16)
        s["a_r"] = jnp.concatenate([jnp.where(incl, G[2 * C:4 * C, 0:2 * C], 0.0),
                                    jnp.where(incl, G[2 * C:4 * C, 2 * C:4 * C], 0.0)], axis=1).astype(BF16)
        s["t_inv"] = jnp.where(eye, 1.0, a_ab)
        s["apow"] = a_ab.astype(BF16)
    for s in st:
        s["akv"] = _dot(s["a_ak"], s["vst_b"]).astype(BF16)
    for s in st:
        s["apow"] = _dot(s["apow"], s["apow"]).astype(BF16)
    for _ in range(4):
        for s in st:
            res = _dot(s["apow"], jnp.concatenate([s["apow"], s["t_inv"].astype(BF16)], axis=1))
            s["apow"] = res[:, 0:2 * C].astype(BF16)
            s["t_inv"] = s["t_inv"] + res[:, 2 * C:4 * C]
    for s in st:
        s["t_inv"] = s["t_inv"] + _dot(s["apow"], s["t_inv"].astype(BF16))
    for s in st:
        pq = _dot(s["t_inv"].astype(BF16), jnp.concatenate([s["ast_b"], s["akv"]], axis=1))
        s["rhs2"] = jnp.concatenate(
            [pq.astype(BF16), jnp.concatenate([jnp.zeros_like(s["vst_b"]), s["vst_b"]], axis=1)], axis=0)
    for s in st:
        ry = _dot(s["a_r"], s["rhs2"])
        rhat = s["rst"] + ry[:, 0:128]
        y0 = ry[:, 128:256]
        r_ref[s["rs"], s["sl"]] = (rhat[0:C, :] + rhat[C:2 * C, :]).astype(BF16)
        y0_ref[s["rs"], s["sl"]] = y0[0:C, :] + y0[C:2 * C, :]
    for (c, j), s in zip(chains, st):
        mn = _dot_tn(s["bkh"], s["rhs2"])
        m_ref[c, j] = (mn[:, 0:128] + jnp.where(eye, s["wc"], 0.0)).astype(BF16)
        n_ref[c, j] = mn[:, 128:256]


def _rwkv_local_call(proj, mu_rkv, mu_lr, wlr, w0, a0, kk, ka, rk, e512, T):
    M = proj.shape[0]
    C = RWKV_CHUNK
    nc = RWKV_CHUNKS_PER_STEP
    R = nc * C
    nchunks = M // C
    ridx = jnp.arange(R)
    tri = ((ridx[:, None] >= ridx[None, :]) & (ridx[:, None] // C == ridx[None, :] // C)).astype(BF16)
    vec = lambda n: pl.BlockSpec((1, n), lambda i: (0, 0))
    rowspec = pl.BlockSpec((R, GW), lambda i: (i, 0))
    matspec = pl.BlockSpec((nc, GH // 2, 128, 128), lambda i: (i, 0, 0, 0))
    prev = lambda i: jnp.maximum(i * (R // 8) - 1, 0)
    return pl.pallas_call(
        functools.partial(_rwkv_local_kernel, chunks_per_seq=T // C, nc=nc),
        grid=(nchunks // nc,),
        in_specs=[
            pl.BlockSpec((R, 3 * GW), lambda i: (i, OFF_RR // (3 * GW))),
            pl.BlockSpec((R, 256), lambda i: (i, OFF_RLR // 256)),
            pl.BlockSpec((8, 3 * GW), lambda i: (prev(i), OFF_RR // (3 * GW))),
            pl.BlockSpec((8, 256), lambda i: (prev(i), OFF_RLR // 256)),
            vec(3 * GW), vec(256),
            pl.BlockSpec((256, 3 * GW), lambda i: (0, 0)),
            vec(GW), vec(GW), vec(GW), vec(GW), vec(GW),
            pl.BlockSpec((GW, GW), lambda i: (0, 0)),
            pl.BlockSpec((R, R), lambda i: (0, 0)),
        ],
        out_specs=[matspec, matspec, rowspec, rowspec, rowspec, rowspec],
        out_shape=[
            jax.ShapeDtypeStruct((nchunks, GH // 2, 128, 128), BF16),
            jax.ShapeDtypeStruct((nchunks, GH // 2, 128, 128), F32),
            jax.ShapeDtypeStruct((M, GW), BF16),
            jax.ShapeDtypeStruct((M, GW), F32),
            jax.ShapeDtypeStruct((M, GW), F32),
            jax.ShapeDtypeStruct((M, GW), F32),
        ],
        compiler_params=_cparams(("parallel",)),
        name="rwkv_chunk_local",
    )(proj, proj, proj, proj, mu_rkv, mu_lr, wlr, w0, a0, kk, ka, rk, e512, tri)


def _rwkv_state_kernel(m_ref, n_ref, r_ref, y0_ref, bonus_ref, g_ref, lg_ref, lb_ref, e_ref,
                       o_ref, h_ref, *, batch, steps):
    @pl.when(pl.program_id(0) == 0)
    def _():
        h_ref[...] = jnp.zeros_like(h_ref)

    C = RWKV_CHUNK
    for c in range(steps):
        rows = slice(c * C, (c + 1) * C)
        ys = {}
        for b in range(batch):
            for j in range(GH // 2):
                sl = slice(128 * j, 128 * (j + 1))
                h = h_ref[b, j].astype(BF16)
                ys[b, j] = _dot(r_ref[b, rows, sl], h) + y0_ref[b, rows, sl]
                h_ref[b, j] = _dot(m_ref[b, c, j], h) + n_ref[b, c, j]
        for b in range(batch):
            y = jnp.concatenate([ys[b, j] for j in range(GH // 2)], axis=1)
            mean = _seg_mean(y, e_ref)
            yc = y - mean
            var = _seg_mean(yc * yc, e_ref)
            yn = yc * lax.rsqrt(var + RWKV_GN_EPS) * lg_ref[...] + lb_ref[...]
            o_ref[b, rows, :] = ((yn + bonus_ref[b, rows, :]) * g_ref[b, rows, :]).astype(BF16)


def _rwkv_state_call(mm, nn, rr, y0, bonus, g, lnx_g, lnx_b, e512, B, T):
    C = RWKV_CHUNK
    steps = RWKV_STATE_CHUNKS_PER_STEP
    nc = T // C
    mm = mm.reshape(B, nc, GH // 2, 128, 128)
    nn = nn.reshape(B, nc, GH // 2, 128, 128)
    r3 = lambda a: a.reshape(B, T, GW)
    matspec = pl.BlockSpec((B, steps, GH // 2, 128, 128), lambda c: (0, c, 0, 0, 0))
    rowspec = pl.BlockSpec((B, steps * C, GW), lambda c: (0, c, 0))
    vec = pl.BlockSpec((1, GW), lambda c: (0, 0))
    out = pl.pallas_call(
        functools.partial(_rwkv_state_kernel, batch=B, steps=steps),
        grid=(nc // steps,),
        in_specs=[matspec, matspec, rowspec, rowspec, rowspec, rowspec, vec, vec,
                  pl.BlockSpec((GW, GW), lambda c: (0, 0))],
        out_specs=rowspec,
        out_shape=jax.ShapeDtypeStruct((B, T, GW), BF16),
        scratch_shapes=[pltpu.VMEM((B, GH // 2, 128, 128), F32)],
        compiler_params=_cparams(("arbitrary",)),
        name="rwkv_state_scan",
    )(mm, nn, r3(rr), r3(y0), r3(bonus), r3(g), lnx_g, lnx_b, e512)
    return out.reshape(B * T, GW)


def _pad_cols(a, n):
    return jnp.pad(a, [(0, 0)] * (a.ndim - 1) + [(0, n - a.shape[-1])])


def _permute_in_cols(w):
    rw, sw, sg, fx = 0, 1696, 2464, 3488
    parts = [
        w[..., fx:fx + 1536],
        w[..., fx + 1544:fx + 2056],
        w[..., sg:sg + 1024],
        w[..., rw:rw + 1536],
        w[..., sw:sw + 768],
        _pad_cols(w[..., rw + 1536:rw + 1696], 256),
        _pad_cols(w[..., fx + 1536:fx + 1544], 128),
    ]
    return jnp.concatenate(parts, axis=-1)


def _tile_heads(g):
    return jnp.tile(g, GH).reshape(1, GW)


def kernel(x, c, w_mod, b_mod, norm1_g, norm2_g, w_in, w_out, rwkv_mu, rwkv_w0, rwkv_w2, rwkv_a0, rwkv_a2, rwkv_g2, rwkv_kk, rwkv_ka, rwkv_rk, rwkv_lnx_g, rwkv_lnx_b, swa_qn_g, swa_kn_g, swa_sinks, sgu_ln_g, sgu_ln_b, sgu_ws, sgu_b, fox_qn_g, fox_kn_g, fox_fb, ffn_w1, ffn_w3, ffn_w2):
    B, T, D = x.shape
    L = w_mod.shape[0]
    M = B * T

    mod = _mod_call(c, w_mod, b_mod)
    mod = mod.reshape(L, B, 6, 1, D)

    seg = jnp.arange(GW) // HEAD_DIM
    e512 = jnp.where(seg[:, None] == seg[None, :], 1.0 / HEAD_DIM, 0.0).astype(BF16)
    rows, cols = np.tril_indices(BLOCK)

    row = lambda a: a.reshape(L, 1, -1)
    tile_heads = lambda g, n: jnp.tile(g, (1, n)).reshape(L, 1, n * HEAD_DIM)
    w_in_p = _permute_in_cols(w_in).astype(BF16)
    mu_rkv = row(rwkv_mu[:, 0:3 * GW])
    mu_lr = row(_pad_cols(rwkv_mu[:, 3 * GW:], 256))
    wlr = jnp.concatenate([
        jnp.pad(rwkv_w2, ((0, 0), (0, 0), (0, 2 * GW))),
        jnp.pad(rwkv_a2, ((0, 0), (0, 0), (GW, GW))),
        jnp.pad(rwkv_g2, ((0, 0), (0, 0), (2 * GW, 0))),
        jnp.zeros((L, 256 - 160, 3 * GW), F32)], axis=1).astype(BF16)
    w_s = jnp.zeros((L, GH, BLOCK, BLOCK), F32).at[:, :, rows, cols].set(sgu_ws)
    ws_cat = jnp.concatenate([w_s[:, 0::2], w_s[:, 1::2]], axis=3).astype(BF16)
    bias_full = jnp.repeat(jnp.swapaxes(sgu_b, 1, 2), HEAD_DIM, axis=2)
    swa_qg, swa_kg = tile_heads(swa_qn_g, GH), tile_heads(swa_kn_g, 2)
    fox_qg, fox_kg = tile_heads(fox_qn_g, GH), tile_heads(fox_kn_g, GH)
    fox_b = row(_pad_cols(fox_fb, 128))

    x2 = x.reshape(M, D)
    for l in range(L):
        shift1, scale1, gate1, shift2, scale2, gate2 = [mod[l, :, s] for s in range(6)]
        proj = _in_call(x2, norm1_g[l].reshape(1, D), scale1, shift1, w_in_p[l], T)

        vec = lambda a: a.reshape(1, GW)
        mm, nn, rr, y0, bonus, g = _rwkv_local_call(
            proj, mu_rkv[l], mu_lr[l], wlr[l], vec(rwkv_w0[l]), vec(rwkv_a0[l]), vec(rwkv_kk[l]),
            vec(rwkv_ka[l]), vec(rwkv_rk[l]), e512, T)
        ya = _rwkv_state_call(mm, nn, rr, y0, bonus, g, vec(rwkv_lnx_g[l]), vec(rwkv_lnx_b[l]), e512, B, T)

        yb = _swa_call(proj, swa_sinks[l], swa_qg[l], swa_kg[l], e512, T)

        yc = _sgu_call(proj, vec(sgu_ln_g[l]), vec(sgu_ln_b[l]), ws_cat[l], bias_full[l])

        qt, ka, vt = _fox_prep_call(proj, fox_qg[l], fox_kg[l], fox_b[l], e512, T)
        yd = _fox_call(proj, qt, ka, vt, T)

        x2 = _out_call(ya, yb, yc, yd, w_out[l].astype(BF16), x2, gate1, T)
        x2 = _ffn_call(x2, norm2_g[l].reshape(1, D), scale2, shift2, gate2,
                       ffn_w1[l].astype(BF16), ffn_w3[l].astype(BF16), ffn_w2[l].astype(BF16), T)
    return x2.reshape(B, T, D)
```

```python
import functools

import numpy as np
import jax
import jax.numpy as jnp
from jax import lax
from jax.experimental import pallas as pl
from jax.experimental.pallas import tpu as pltpu

F32 = jnp.float32
BF16 = jnp.bfloat16

D_MODEL = 2048
HEAD_DIM = 64
GW = 512
GH = 8
BLOCK = 128
NORM_EPS = 1e-6
LN_EPS = 1e-5
NEG_INF = -1e30
RWKV_GN_EPS = 64e-5
NORM_ROWS = 16
SWA_GROUPS = 4
FOX_SKEW = (3, 6)
RWKV_CHUNK = 64
RWKV_CHUNKS_PER_STEP = 4
RWKV_STATE_CHUNKS_PER_STEP = 2
LOG2E = float(np.log2(np.e))
FFN_HIDDEN = 5632

OFF_FQ, OFF_FK, OFF_FV, OFF_FG = 0, 512, 1024, 1536
OFF_SU, OFF_SV = 2048, 2560
OFF_RR = 3072
OFF_WQ = 4608
OFF_WKV = 5120
OFF_RLR = 5376
OFF_FF = 5632
NP = 5760

VMEM_LIMIT = 56 * 1024 * 1024


def _cparams(sem):
    return pltpu.CompilerParams(dimension_semantics=sem, vmem_limit_bytes=VMEM_LIMIT)


def _dot(a, b):
    return jnp.dot(a, b, preferred_element_type=F32)


def _dot_nt(a, b):
    return lax.dot_general(a, b, (((1,), (1,)), ((), ())), preferred_element_type=F32)


def _dot_tn(a, b):
    return lax.dot_general(a, b, (((0,), (0,)), ((), ())), preferred_element_type=F32)


def _seg_mean(x, e_ref):
    return _dot(x.astype(BF16), e_ref[...])


def _split3(x):
    hi = x.astype(BF16)
    r1 = x - hi.astype(F32)
    mid = r1.astype(BF16)
    lo = (r1 - mid.astype(F32)).astype(BF16)
    return hi, mid, lo


def _half_masks(shape):
    lane = lax.broadcasted_iota(jnp.int32, shape, len(shape) - 1)
    m0 = (lane % 128) < 64
    return m0


def _mod_kernel(c_ref, w_ref, b_ref, o_ref):
    c = c_ref[...]
    ca = (c * jax.nn.sigmoid(c)).astype(BF16)
    o_ref[0] = _dot(ca, w_ref[0].astype(BF16)) + b_ref[0]


def _mod_call(c, w_mod, b_mod):
    L, D, N = w_mod.shape
    B = c.shape[0]
    tn = 1024
    return pl.pallas_call(
        _mod_kernel,
        grid=(L, N // tn),
        in_specs=[
            pl.BlockSpec((B, D), lambda l, j: (0, 0)),
            pl.BlockSpec((1, D, tn), lambda l, j: (l, 0, j)),
            pl.BlockSpec((1, 1, tn), lambda l, j: (l, 0, j)),
        ],
        out_specs=pl.BlockSpec((1, B, tn), lambda l, j: (l, 0, j)),
        out_shape=jax.ShapeDtypeStruct((L, B, N), F32),
        compiler_params=_cparams(("parallel", "parallel")),
        name="adaln_mod",
    )(c, w_mod, b_mod.reshape(L, 1, N))


def _norm_modulate(x_ref, g_ref, sc_ref, sh_ref, h_ref):
    gs = g_ref[...] * (1.0 + sc_ref[0])
    sh = sh_ref[0]

    def body(c, carry):
        rows = pl.ds(pl.multiple_of(c * NORM_ROWS, NORM_ROWS), NORM_ROWS)
        x = x_ref[rows, :]
        ms = jnp.mean(x * x, axis=-1, keepdims=True)
        h_ref[rows, :] = (x * lax.rsqrt(ms + NORM_EPS) * gs + sh).astype(BF16)
        return carry

    lax.fori_loop(0, x_ref.shape[0] // NORM_ROWS, body, 0, unroll=8)


def _in_kernel(x_ref, g_ref, sc_ref, sh_ref, w_ref, o_ref, h_ref):
    @pl.when(pl.program_id(1) == 0)
    def _():
        _norm_modulate(x_ref, g_ref, sc_ref, sh_ref, h_ref)

    o_ref[...] = _dot(h_ref[...], w_ref[...])


def _in_call(x2, g, scale, shift, w, T):
    M, D = x2.shape
    n = w.shape[1]
    tm, tn = 1024, 1920
    return pl.pallas_call(
        _in_kernel,
        grid=(M // tm, n // tn),
        in_specs=[
            pl.BlockSpec((tm, D), lambda i, j: (i, 0)),
            pl.BlockSpec((1, D), lambda i, j: (0, 0)),
            pl.BlockSpec((1, 1, D), lambda i, j: (i * tm // T, 0, 0)),
            pl.BlockSpec((1, 1, D), lambda i, j: (i * tm // T, 0, 0)),
            pl.BlockSpec((D, tn), lambda i, j: (0, j)),
        ],
        out_specs=pl.BlockSpec((tm, tn), lambda i, j: (i, j)),
        out_shape=jax.ShapeDtypeStruct((M, n), F32),
        scratch_shapes=[pltpu.VMEM((tm, D), BF16)],
        compiler_params=_cparams(("parallel", "arbitrary")),
        name="norm_in_proj",
    )(x2, g, scale, shift, w)


def _out_kernel(ya_ref, yb_ref, yc_ref, yd_ref, w_ref, x_ref, gt_ref, o_ref):
    acc = _dot(ya_ref[...], w_ref[0:GW, :])
    acc += _dot(yb_ref[...], w_ref[GW:2 * GW, :])
    acc += _dot(yc_ref[...], w_ref[2 * GW:3 * GW, :])
    acc += _dot(yd_ref[...], w_ref[3 * GW:4 * GW, :])
    o_ref[...] = x_ref[...] + gt_ref[0] * acc


def _out_call(ya, yb, yc, yd, w, x2, gate, T):
    M, D = x2.shape
    tm = 512
    yspec = pl.BlockSpec((tm, GW), lambda i: (i, 0))
    return pl.pallas_call(
        _out_kernel,
        grid=(M // tm,),
        in_specs=[
            yspec, yspec, yspec, yspec,
            pl.BlockSpec((D, D), lambda i: (0, 0)),
            pl.BlockSpec((tm, D), lambda i: (i, 0)),
            pl.BlockSpec((1, 1, D), lambda i: (i * tm // T, 0, 0)),
        ],
        out_specs=pl.BlockSpec((tm, D), lambda i: (i, 0)),
        out_shape=jax.ShapeDtypeStruct((M, D), F32),
        compiler_params=_cparams(("parallel",)),
        name="out_proj_residual",
    )(ya, yb, yc, yd, w, x2, gate)


def _ffn_kernel(x_ref, g_ref, sc_ref, sh_ref, gt_ref, w1_ref, w3_ref, w2_ref, o_ref, h_ref):
    @pl.when(pl.program_id(1) == 0)
    def _():
        _norm_modulate(x_ref, g_ref, sc_ref, sh_ref, h_ref)
        o_ref[...] = x_ref[...]

    h = h_ref[...]
    half = w1_ref.shape[1] // 2
    fs = []
    for c in range(2):
        cols = slice(c * half, (c + 1) * half)
        a = _dot(h, w1_ref[:, cols])
        b = _dot(h, w3_ref[:, cols])
        fs.append((a * jax.nn.sigmoid(a) * b).astype(BF16))
    o_ref[...] += gt_ref[0] * _dot(jnp.concatenate(fs, axis=1), w2_ref[...])


def _ffn_call(x2, g, scale, shift, gate, w1, w3, w2, T):
    M, D = x2.shape
    F = w1.shape[1]
    tm, tf = 1024, 512
    bspec = pl.BlockSpec((1, 1, D), lambda i, j: (i * tm // T, 0, 0))
    return pl.pallas_call(
        _ffn_kernel,
        grid=(M // tm, F // tf),
        in_specs=[
            pl.BlockSpec((tm, D), lambda i, j: (i, 0)),
            pl.BlockSpec((1, D), lambda i, j: (0, 0)),
            bspec, bspec, bspec,
            pl.BlockSpec((D, tf), lambda i, j: (0, j)),
            pl.BlockSpec((D, tf), lambda i, j: (0, j)),
            pl.BlockSpec((tf, D), lambda i, j: (j, 0)),
        ],
        out_specs=pl.BlockSpec((tm, D), lambda i, j: (i, 0)),
        out_shape=jax.ShapeDtypeStruct((M, D), F32),
        scratch_shapes=[pltpu.VMEM((tm, D), BF16)],
        compiler_params=_cparams(("parallel", "arbitrary")),
        name="swiglu_ffn",
    )(x2, g, scale, shift, gate, w1, w3, w2)


def _gelu(x):
    return 0.5 * x * (1.0 + lax.erf(x * np.float32(1.0 / np.sqrt(2.0))))


def _sgu_kernel(u_ref, v_ref, g_ref, b_ref, ws_ref, bias_ref, o_ref, *, chunks):
    m0 = _half_masks((BLOCK, 128))
    for c in range(chunks):
        rows = slice(c * BLOCK, (c + 1) * BLOCK)
        u = _gelu(u_ref[rows, :])
        v = _gelu(v_ref[rows, :])
        mu = jnp.mean(v, axis=-1, keepdims=True)
        vc = v - mu
        var = jnp.mean(vc * vc, axis=-1, keepdims=True)
        vn = vc * lax.rsqrt(var + LN_EPS) * g_ref[...] + b_ref[...]
        outs = []
        for j in range(GH // 2):
            vp = vn[:, 128 * j:128 * (j + 1)]
            stacked = jnp.concatenate(
                [jnp.where(m0, vp, 0.0), jnp.where(m0, 0.0, vp)], axis=0).astype(BF16)
            outs.append(_dot(ws_ref[j], stacked))
        z = jnp.concatenate(outs, axis=1) + bias_ref[...]
        o_ref[rows, :] = (u * z).astype(BF16)


def _sgu_call(proj, ln_g, ln_b, ws_cat, bias_full):
    M = proj.shape[0]
    chunks = 4
    tr = chunks * BLOCK
    return pl.pallas_call(
        functools.partial(_sgu_kernel, chunks=chunks),
        grid=(M // tr,),
        in_specs=[
            pl.BlockSpec((tr, GW), lambda i: (i, OFF_SU // GW)),
            pl.BlockSpec((tr, GW), lambda i: (i, OFF_SV // GW)),
            pl.BlockSpec((1, GW), lambda i: (0, 0)),
            pl.BlockSpec((1, GW), lambda i: (0, 0)),
            pl.BlockSpec((GH // 2, BLOCK, 2 * BLOCK), lambda i: (0, 0, 0)),
            pl.BlockSpec((BLOCK, GW), lambda i: (0, 0)),
        ],
        out_specs=pl.BlockSpec((tr, GW), lambda i: (i, 0)),
        out_shape=jax.ShapeDtypeStruct((M, GW), BF16),
        compiler_params=_cparams(("parallel",)),
        name="sgu_mix",
    )(proj, proj, ln_g, ln_b, ws_cat, bias_full)


def _swa_kernel(sink_ref, q_ref, kv_ref, kvp_ref, qg_ref, kg_ref, e_ref, o_ref, *, blocks, blocks_per_seq):
    i = pl.program_id(0)
    m0q = _half_masks((BLOCK, 128))
    m0k = _half_masks((2 * BLOCK, 128))
    row = lax.broadcasted_iota(jnp.int32, (BLOCK, 2 * BLOCK), 0)
    col = lax.broadcasted_iota(jnp.int32, (BLOCK, 2 * BLOCK), 1)
    band = (col <= row + BLOCK) & (col > row)
    e128 = e_ref.at[0:128, 0:128]

    kv_all = jnp.concatenate([kvp_ref[...], kv_ref[...]], axis=0)
    for r in range(blocks):
        first = ((i * blocks + r) % blocks_per_seq) == 0
        mask = band & (col >= jnp.where(first, BLOCK, 0))
        q = q_ref[r * BLOCK:(r + 1) * BLOCK, :]
        qn = q * lax.rsqrt(_seg_mean(q * q, e_ref) + NORM_EPS) * qg_ref[...] * np.float32(HEAD_DIM ** -0.5 * LOG2E)
        kv = kv_all[r * BLOCK:(r + 2) * BLOCK, :]
        k = kv[:, 0:128]
        v = kv[:, 128:256]
        kn = k * lax.rsqrt(_seg_mean(k * k, e128) + NORM_EPS) * kg_ref[...]
        kn_sw = pltpu.roll(kn, 64, axis=1)
        v_sw = pltpu.roll(v, 64, axis=1)
        kdup = [jnp.where(m0k, kn, kn_sw).astype(BF16), jnp.where(m0k, kn_sw, kn).astype(BF16)]
        vdup = [jnp.where(m0k, v, v_sw), jnp.where(m0k, v_sw, v)]
        vmsk = [[jnp.where(m0k, vd, 0.0).astype(BF16), jnp.where(m0k, 0.0, vd).astype(BF16)] for vd in vdup]
        scores = []
        for h in range(GH):
            qp = qn[:, 128 * (h // 2):128 * (h // 2 + 1)]
            qm = (jnp.where(m0q, qp, 0.0) if h % 2 == 0 else jnp.where(m0q, 0.0, qp)).astype(BF16)
            scores.append(_dot_nt(qm, kdup[h // SWA_GROUPS]))
        probs, inv = [], []
        for h in range(GH):
            s = jnp.where(mask, scores[h], NEG_INF)
            sink = sink_ref[h] * np.float32(LOG2E)
            m = jnp.maximum(jnp.max(s, axis=-1, keepdims=True), sink)
            p = jnp.exp2(s - m)
            inv.append(1.0 / (jnp.sum(p, axis=-1, keepdims=True) + jnp.exp2(sink - m)))
            probs.append(p.astype(BF16))
        outs = []
        for j in range(GH // 2):
            g = (2 * j) // SWA_GROUPS
            outs.append(_dot(probs[2 * j], vmsk[g][0]) * inv[2 * j]
                        + _dot(probs[2 * j + 1], vmsk[g][1]) * inv[2 * j + 1])
        o_ref[r * BLOCK:(r + 1) * BLOCK, :] = jnp.concatenate(outs, axis=1).astype(BF16)


def _swa_call(proj, sinks, qg, kg, e512, T):
    M = proj.shape[0]
    blocks = 2
    tr = blocks * BLOCK
    return pl.pallas_call(
        functools.partial(_swa_kernel, blocks=blocks, blocks_per_seq=T // BLOCK),
        grid=(M // tr,),
        in_specs=[
            pl.BlockSpec(memory_space=pltpu.SMEM),
            pl.BlockSpec((tr, GW), lambda i: (i, OFF_WQ // GW)),
            pl.BlockSpec((tr, 256), lambda i: (i, OFF_WKV // 256)),
            pl.BlockSpec((BLOCK, 256), lambda i: (jnp.maximum(i * blocks - 1, 0), OFF_WKV // 256)),
            pl.BlockSpec((1, GW), lambda i: (0, 0)),
            pl.BlockSpec((1, 128), lambda i: (0, 0)),
            pl.BlockSpec((GW, GW), lambda i: (0, 0)),
        ],
        out_specs=pl.BlockSpec((tr, GW), lambda i: (i, 0)),
        out_shape=jax.ShapeDtypeStruct((M, GW), BF16),
        compiler_params=_cparams(("parallel",)),
        name="swa_mix",
    )(sinks, proj, proj, proj, qg, kg, e512)


def _fox_prep_kernel(q_ref, k_ref, v_ref, f_ref, qg_ref, kg_ref, fb_ref, e_ref, tri_ref, sel_ref, ones_ref,
                     qt_ref, ka_ref, vt_ref, carry_ref):
    @pl.when(pl.program_id(1) == 0)
    def _():
        carry_ref[...] = jnp.zeros_like(carry_ref)

    q = q_ref[...]
    k = k_ref[...]
    v = v_ref[...]
    tr = q.shape[0]
    qn = q * lax.rsqrt(_seg_mean(q * q, e_ref) + NORM_EPS) * qg_ref[...] * np.float32(HEAD_DIM ** -0.5 * LOG2E)
    kn = k * lax.rsqrt(_seg_mean(k * k, e_ref) + NORM_EPS) * kg_ref[...]
    z = f_ref[...] + fb_ref[...]
    logf2 = (jnp.minimum(z, 0.0) - jnp.log1p(jnp.exp(-jnp.abs(z)))) * np.float32(LOG2E)
    tri = tri_ref[...]
    cum = sum(_dot(tri, part) for part in _split3(logf2)) + carry_ref[...]
    carry_ref[...] = cum[tr - 1:tr, :]
    hi, mid, lo = _split3(-cum)
    bias = _dot(jnp.concatenate([hi, mid, lo], axis=1), sel_ref[...])
    m0 = _half_masks((tr, 128))
    for h in range(GH):
        p, half = divmod(h, 2)
        slot = slice(128 * h, 128 * (h + 1))
        pair = slice(128 * p, 128 * (p + 1))
        own_k, own_q, other_k, other_q = kn[:, pair], qn[:, pair], bias[:, slot], ones_ref[:, slot]
        if half == 0:
            ka = jnp.where(m0, own_k, other_k)
            qa = jnp.where(m0, own_q, other_q)
        else:
            ka = jnp.where(m0, other_k, own_k)
            qa = jnp.where(m0, other_q, own_q)
        ka_ref[:, slot] = ka.astype(BF16)
        qt_ref[0, slot, :] = qa.T.astype(BF16)
    for p in range(GH // 2):
        pair = slice(128 * p, 128 * (p + 1))
        vt_ref[0, pair, :] = v[:, pair].T.astype(BF16)


def _fox_prep_call(proj, qg, kg, fb, e512, T):
    M = proj.shape[0]
    B = M // T
    tr = 256
    nb = T // tr
    tri = jnp.tril(jnp.ones((tr, tr), BF16))
    sel = np.zeros((3 * 128, GH * 128), np.float32)
    ones = np.zeros((1, GH * 128), np.float32)
    for h in range(GH):
        off = 128 * h + (64 if h % 2 == 0 else 0)
        for part in range(3):
            sel[128 * part + h, off + part] = 1.0
            ones[0, off + part] = 1.0
    cspec = lambda c: pl.BlockSpec((tr, GW), lambda b, i, c=c: (b * nb + i, c))
    return pl.pallas_call(
        _fox_prep_kernel,
        grid=(B, nb),
        in_specs=[
            cspec(OFF_FQ // GW), cspec(OFF_FK // GW), cspec(OFF_FV // GW),
            pl.BlockSpec((tr, 128), lambda b, i: (b * nb + i, OFF_FF // 128)),
            pl.BlockSpec((1, GW), lambda b, i: (0, 0)),
            pl.BlockSpec((1, GW), lambda b, i: (0, 0)),
            pl.BlockSpec((1, 128), lambda b, i: (0, 0)),
            pl.BlockSpec((GW, GW), lambda b, i: (0, 0)),
            pl.BlockSpec((tr, tr), lambda b, i: (0, 0)),
            pl.BlockSpec((3 * 128, GH * 128), lambda b, i: (0, 0)),
            pl.BlockSpec((1, GH * 128), lambda b, i: (0, 0)),
        ],
        out_specs=[
            pl.BlockSpec((1, GH * 128, tr), lambda b, i: (b, 0, i)),
            pl.BlockSpec((tr, GH * 128), lambda b, i: (b * nb + i, 0)),
            pl.BlockSpec((1, GW, tr), lambda b, i: (b, 0, i)),
        ],
        out_shape=[
            jax.ShapeDtypeStruct((B, GH * 128, T), BF16),
            jax.ShapeDtypeStruct((M, GH * 128), BF16),
            jax.ShapeDtypeStruct((B, GW, T), BF16),
        ],
        scratch_shapes=[pltpu.VMEM((1, 128), F32)],
        compiler_params=_cparams(("parallel", "arbitrary")),
        name="fox_prep",
    )(proj, proj, proj, proj, qg, kg, fb, e512, tri, jnp.asarray(sel, BF16), jnp.asarray(ones, F32))


def _fox_kernel(qt_ref, ka_ref, vt_ref, gl_ref, o_ref, m_ref, l_ref, acc_ref, *, tq):
    i = pl.program_id(1)
    m_ref[...] = jnp.full(m_ref.shape, NEG_INF, F32)
    l_ref[...] = jnp.zeros(l_ref.shape, F32)
    acc_ref[...] = jnp.zeros(acc_ref.shape, F32)

    def step(start, nk, masked):
        start = pl.multiple_of(start, tq)
        if masked:
            key = lax.broadcasted_iota(jnp.int32, (nk, tq), 0) + start
            qry = lax.broadcasted_iota(jnp.int32, (nk, tq), 1) + i * tq
            keep = key <= qry
        sts, pts, alphas = {}, {}, {}
        d1, d2 = FOX_SKEW
        for t in range(GH + d2):
            if t < GH:
                slot = slice(128 * t, 128 * (t + 1))
                sts[t] = _dot(ka_ref[pl.ds(start, nk), slot], qt_ref[0, slot, :])
            if d1 <= t < GH + d1:
                h = t - d1
                st = jnp.where(keep, sts.pop(h), NEG_INF) if masked else sts.pop(h)
                m_old = m_ref[h]
                m_new = jnp.maximum(m_old, jnp.max(st, axis=0, keepdims=True))
                alphas[h] = jnp.exp2(m_old - m_new)
                pt = jnp.exp2(st - m_new)
                l_ref[h] = alphas[h] * l_ref[h] + jnp.sum(pt, axis=0, keepdims=True)
                m_ref[h] = m_new
                pts[h] = pt.astype(BF16)
            if t >= d2:
                h = t - d2
                vth = vt_ref[0, 64 * h:64 * (h + 1), pl.ds(start, nk)]
                acc_ref[h] = alphas.pop(h) * acc_ref[h] + _dot(vth, pts.pop(h))

    def body(kb, carry):
        step(kb * (2 * tq), 2 * tq, False)
        return carry

    lax.fori_loop(0, i // 2, body, 0)

    @pl.when(i % 2 == 1)
    def _():
        step((i - 1) * tq, 2 * tq, True)

    @pl.when(i % 2 == 0)
    def _():
        step(i * tq, tq, True)

    for p in range(GH // 2):
        pair = slice(128 * p, 128 * (p + 1))
        ot = jnp.concatenate([acc_ref[2 * p] / l_ref[2 * p], acc_ref[2 * p + 1] / l_ref[2 * p + 1]], axis=0)
        o_ref[:, pair] = (ot.T * jax.nn.sigmoid(gl_ref[:, pair])).astype(BF16)


def _fox_call(proj, qt, ka, vt, T):
    M = proj.shape[0]
    B = M // T
    tq = 256
    nq = T // tq
    return pl.pallas_call(
        functools.partial(_fox_kernel, tq=tq),
        grid=(B, nq),
        in_specs=[
            pl.BlockSpec((1, GH * 128, tq), lambda b, i: (b, 0, i)),
            pl.BlockSpec((T, GH * 128), lambda b, i: (b, 0)),
            pl.BlockSpec((1, GW, T), lambda b, i: (b, 0, 0)),
            pl.BlockSpec((tq, GW), lambda b, i: (b * nq + i, OFF_FG // GW)),
        ],
        out_specs=pl.BlockSpec((tq, GW), lambda b, i: (b * nq + i, 0)),
        out_shape=jax.ShapeDtypeStruct((M, GW), BF16),
        scratch_shapes=[
            pltpu.VMEM((GH, 1, tq), F32),
            pltpu.VMEM((GH, 1, tq), F32),
            pltpu.VMEM((GH, HEAD_DIM, tq), F32),
        ],
        compiler_params=_cparams(("parallel", "arbitrary")),
        name="fox_attn",
    )(qt, ka, vt, proj)


def _rwkv_local_kernel(p_ref, lr_ref, pp_ref, lrp_ref, mu_ref, mulr_ref, wlr_ref, w0_ref, a0_ref,
                       kk_ref, ka_ref, rk_ref, e_ref, tri_ref,
                       m_ref, n_ref, r_ref, y0_ref, bonus_ref, g_ref, *, chunks_per_seq, nc):
    C = RWKV_CHUNK
    i = pl.program_id(0)
    has_prev = (((i * nc) % chunks_per_seq) != 0).astype(F32)
    row = lax.broadcasted_iota(jnp.int32, (nc * C, 1), 0)

    def shifted(cur_ref, prev_ref, mu):
        cur = cur_ref[...]
        prev = jnp.where(row == 0, prev_ref[7:8, :] * has_prev, pltpu.roll(cur, 1, axis=0))
        return cur + (prev - cur) * mu

    p = shifted(p_ref, pp_ref, mu_ref[...])
    lr = shifted(lr_ref, lrp_ref, mulr_ref[...])
    r = p[:, 0:GW]
    k = p[:, GW:2 * GW]
    v = p[:, 2 * GW:3 * GW]

    lane = lax.broadcasted_iota(jnp.int32, lr.shape, 1)
    z = jnp.where(lane < 32, jnp.tanh(lr), jnp.where(lane < 64, lr, jax.nn.sigmoid(lr)))
    low = _dot(z.astype(BF16), wlr_ref[...])
    lw = -np.float32(np.exp(-0.5)) * jax.nn.sigmoid(w0_ref[...] + low[:, 0:GW])
    a = jax.nn.sigmoid(a0_ref[...] + low[:, GW:2 * GW])
    g_ref[...] = low[:, 2 * GW:3 * GW]

    kk = k * kk_ref[...]
    n2 = _seg_mean(kk * kk, e_ref) * np.float32(HEAD_DIM)
    kk = kk * lax.rsqrt(jnp.maximum(n2, 1e-24))
    k = k * (1.0 + (a - 1.0) * ka_ref[...])
    b = kk * a
    bonus_ref[...] = _seg_mean(r * k * rk_ref[...], e_ref) * np.float32(HEAD_DIM) * v

    tri = tri_ref[...]
    lw_hi = lw.astype(BF16)
    lw_lo = (lw - lw_hi.astype(F32)).astype(BF16)
    L_all = _dot(tri, lw_hi) + _dot(tri, lw_lo)

    m0 = _half_masks((C, 128))
    ri = lax.broadcasted_iota(jnp.int32, (2 * C, 2 * C), 0)
    ci = lax.broadcasted_iota(jnp.int32, (2 * C, 2 * C), 1)
    strict = (ri % C) > (ci % C)
    incl = (ri % C) >= (ci % C)
    eye = ri == ci

    def stack(xp):
        return jnp.concatenate([jnp.where(m0, xp, 0.0), jnp.where(m0, 0.0, xp)], axis=0)

    chains = [(c, j) for c in range(nc) for j in range(GH // 2)]
    st = []
    for c, j in chains:
        rs = slice(c * C, (c + 1) * C)
        sl = slice(128 * j, 128 * (j + 1))
        L = L_all[rs, sl]
        Lend = L[C - 1:C, :]
        e_pos = jnp.exp(L)
        e_neg = jnp.exp(-L)
        e_hat = jnp.exp(Lend - L)
        rc, kc, bc = r[rs, sl], k[rs, sl], b[rs, sl]
        rst = stack(rc * e_pos)
        ast_b = stack(-kk[rs, sl] * jnp.exp(L - lw[rs, sl])).astype(BF16)
        st.append(dict(
            rs=rs, sl=sl, rst=rst, ast_b=ast_b, vst_b=stack(v[rs, sl]).astype(BF16),
            left=jnp.concatenate([ast_b, rst.astype(BF16)], axis=0),
            right=jnp.concatenate([stack(bc * e_neg), stack(kc * e_neg)], axis=0).astype(BF16),
            bkh=jnp.concatenate([stack(bc * e_hat), stack(kc * e_hat)], axis=0).astype(BF16),
            wc=jnp.exp(Lend)))
    for s in st:
        G = _dot_nt(s["left"], s["right"])
        a_ab = jnp.where(strict, G[0:2 * C, 0:2 * C], 0.0)
        s["a_ak"] = jnp.where(strict, G[0:2 * C, 2 * C:4 * C], 0.0).astype(BF16)
        s["a_r"] = jnp.concatenate([jnp.where(incl, G[2 * C:4 * C, 0:2 * C], 0.0),
                                    jnp.where(incl, G[2 * C:4 * C, 2 * C:4 * C], 0.0)], axis=1).astype(BF16)
        s["t_inv"] = jnp.where(eye, 1.0, a_ab)
        s["apow"] = a_ab.astype(BF16)
    for s in st:
        s["akv"] = _dot(s["a_ak"], s["vst_b"]).astype(BF16)
    for s in st:
        s["apow"] = _dot(s["apow"], s["apow"]).astype(BF16)
    for _ in range(4):
        for s in st:
            res = _dot(s["apow"], jnp.concatenate([s["apow"], s["t_inv"].astype(BF16)], axis=1))
            s["apow"] = res[:, 0:2 * C].astype(BF16)
            s["t_inv"] = s["t_inv"] + res[:, 2 * C:4 * C]
    for s in st:
        s["t_inv"] = s["t_inv"] + _dot(s["apow"], s["t_inv"].astype(BF16))
    for s in st:
        pq = _dot(s["t_inv"].astype(BF16), jnp.concatenate([s["ast_b"], s["akv"]], axis=1))
        s["rhs2"] = jnp.concatenate(
            [pq.astype(BF16), jnp.concatenate([jnp.zeros_like(s["vst_b"]), s["vst_b"]], axis=1)], axis=0)
    for s in st:
        ry = _dot(s["a_r"], s["rhs2"])
        rhat = s["rst"] + ry[:, 0:128]
        y0 = ry[:, 128:256]
        r_ref[s["rs"], s["sl"]] = (rhat[0:C, :] + rhat[C:2 * C, :]).astype(BF16)
        y0_ref[s["rs"], s["sl"]] = y0[0:C, :] + y0[C:2 * C, :]
    for (c, j), s in zip(chains, st):
        mn = _dot_tn(s["bkh"], s["rhs2"])
        m_ref[c, j] = (mn[:, 0:128] + jnp.where(eye, s["wc"], 0.0)).astype(BF16)
        n_ref[c, j] = mn[:, 128:256]


def _rwkv_local_call(proj, mu_rkv, mu_lr, wlr, w0, a0, kk, ka, rk, e512, T):
    M = proj.shape[0]
    C = RWKV_CHUNK
    nc = RWKV_CHUNKS_PER_STEP
    R = nc * C
    nchunks = M // C
    ridx = jnp.arange(R)
    tri = ((ridx[:, None] >= ridx[None, :]) & (ridx[:, None] // C == ridx[None, :] // C)).astype(BF16)
    vec = lambda n: pl.BlockSpec((1, n), lambda i: (0, 0))
    rowspec = pl.BlockSpec((R, GW), lambda i: (i, 0))
    matspec = pl.BlockSpec((nc, GH // 2, 128, 128), lambda i: (i, 0, 0, 0))
    prev = lambda i: jnp.maximum(i * (R // 8) - 1, 0)
    return pl.pallas_call(
        functools.partial(_rwkv_local_kernel, chunks_per_seq=T // C, nc=nc),
        grid=(nchunks // nc,),
        in_specs=[
            pl.BlockSpec((R, 3 * GW), lambda i: (i, OFF_RR // (3 * GW))),
            pl.BlockSpec((R, 256), lambda i: (i, OFF_RLR // 256)),
            pl.BlockSpec((8, 3 * GW), lambda i: (prev(i), OFF_RR // (3 * GW))),
            pl.BlockSpec((8, 256), lambda i: (prev(i), OFF_RLR // 256)),
            vec(3 * GW), vec(256),
            pl.BlockSpec((256, 3 * GW), lambda i: (0, 0)),
            vec(GW), vec(GW), vec(GW), vec(GW), vec(GW),
            pl.BlockSpec((GW, GW), lambda i: (0, 0)),
            pl.BlockSpec((R, R), lambda i: (0, 0)),
        ],
        out_specs=[matspec, matspec, rowspec, rowspec, rowspec, rowspec],
        out_shape=[
            jax.ShapeDtypeStruct((nchunks, GH // 2, 128, 128), BF16),
            jax.ShapeDtypeStruct((nchunks, GH // 2, 128, 128), F32),
            jax.ShapeDtypeStruct((M, GW), BF16),
            jax.ShapeDtypeStruct((M, GW), F32),
            jax.ShapeDtypeStruct((M, GW), F32),
            jax.ShapeDtypeStruct((M, GW), F32),
        ],
        compiler_params=_cparams(("parallel",)),
        name="rwkv_chunk_local",
    )(proj, proj, proj, proj, mu_rkv, mu_lr, wlr, w0, a0, kk, ka, rk, e512, tri)


def _rwkv_state_kernel(m_ref, n_ref, r_ref, y0_ref, bonus_ref, g_ref, lg_ref, lb_ref, e_ref,
                       o_ref, h_ref, *, batch, steps):
    @pl.when(pl.program_id(0) == 0)
    def _():
        h_ref[...] = jnp.zeros_like(h_ref)

    C = RWKV_CHUNK
    for c in range(steps):
        rows = slice(c * C, (c + 1) * C)
        ys = {}
        for b in range(batch):
            for j in range(GH // 2):
                sl = slice(128 * j, 128 * (j + 1))
                h = h_ref[b, j].astype(BF16)
                ys[b, j] = _dot(r_ref[b, rows, sl], h) + y0_ref[b, rows, sl]
                h_ref[b, j] = _dot(m_ref[b, c, j], h) + n_ref[b, c, j]
        for b in range(batch):
            y = jnp.concatenate([ys[b, j] for j in range(GH // 2)], axis=1)
            mean = _seg_mean(y, e_ref)
            yc = y - mean
            var = _seg_mean(yc * yc, e_ref)
            yn = yc * lax.rsqrt(var + RWKV_GN_EPS) * lg_ref[...] + lb_ref[...]
            o_ref[b, rows, :] = ((yn + bonus_ref[b, rows, :]) * g_ref[b, rows, :]).astype(BF16)


def _rwkv_state_call(mm, nn, rr, y0, bonus, g, lnx_g, lnx_b, e512, B, T):
    C = RWKV_CHUNK
    steps = RWKV_STATE_CHUNKS_PER_STEP
    nc = T // C
    mm = mm.reshape(B, nc, GH // 2, 128, 128)
    nn = nn.reshape(B, nc, GH // 2, 128, 128)
    r3 = lambda a: a.reshape(B, T, GW)
    matspec = pl.BlockSpec((B, steps, GH // 2, 128, 128), lambda c: (0, c, 0, 0, 0))
    rowspec = pl.BlockSpec((B, steps * C, GW), lambda c: (0, c, 0))
    vec = pl.BlockSpec((1, GW), lambda c: (0, 0))
    out = pl.pallas_call(
        functools.partial(_rwkv_state_kernel, batch=B, steps=steps),
        grid=(nc // steps,),
        in_specs=[matspec, matspec, rowspec, rowspec, rowspec, rowspec, vec, vec,
                  pl.BlockSpec((GW, GW), lambda c: (0, 0))],
        out_specs=rowspec,
        out_shape=jax.ShapeDtypeStruct((B, T, GW), BF16),
        scratch_shapes=[pltpu.VMEM((B, GH // 2, 128, 128), F32)],
        compiler_params=_cparams(("arbitrary",)),
        name="rwkv_state_scan",
    )(mm, nn, r3(rr), r3(y0), r3(bonus), r3(g), lnx_g, lnx_b, e512)
    return out.reshape(B * T, GW)


def _pad_cols(a, n):
    return jnp.pad(a, [(0, 0)] * (a.ndim - 1) + [(0, n - a.shape[-1])])


def _permute_in_cols(w):
    rw, sw, sg, fx = 0, 1696, 2464, 3488
    parts = [
        w[..., fx:fx + 1536],
        w[..., fx + 1544:fx + 2056],
        w[..., sg:sg + 1024],
        w[..., rw:rw + 1536],
        w[..., sw:sw + 768],
        _pad_cols(w[..., rw + 1536:rw + 1696], 256),
        _pad_cols(w[..., fx + 1536:fx + 1544], 128),
    ]
    return jnp.concatenate(parts, axis=-1)


def _tile_heads(g):
    return jnp.tile(g, GH).reshape(1, GW)


def kernel(x, c, w_mod, b_mod, norm1_g, norm2_g, w_in, w_out, rwkv_mu, rwkv_w0, rwkv_w2, rwkv_a0, rwkv_a2, rwkv_g2, rwkv_kk, rwkv_ka, rwkv_rk, rwkv_lnx_g, rwkv_lnx_b, swa_qn_g, swa_kn_g, swa_sinks, sgu_ln_g, sgu_ln_b, sgu_ws, sgu_b, fox_qn_g, fox_kn_g, fox_fb, ffn_w1, ffn_w3, ffn_w2):
    B, T, D = x.shape
    L = w_mod.shape[0]
    M = B * T

    mod = _mod_call(c, w_mod, b_mod)
    mod = mod.reshape(L, B, 6, 1, D)

    seg = jnp.arange(GW) // HEAD_DIM
    e512 = jnp.where(seg[:, None] == seg[None, :], 1.0 / HEAD_DIM, 0.0).astype(BF16)
    rows, cols = np.tril_indices(BLOCK)

    row = lambda a: a.reshape(L, 1, -1)
    tile_heads = lambda g, n: jnp.tile(g, (1, n)).reshape(L, 1, n * HEAD_DIM)
    mu_rkv = row(rwkv_mu[:, 0:3 * GW])
    mu_lr = row(_pad_cols(rwkv_mu[:, 3 * GW:], 256))
    wlr = jnp.concatenate([
        jnp.pad(rwkv_w2, ((0, 0), (0, 0), (0, 2 * GW))),
        jnp.pad(rwkv_a2, ((0, 0), (0, 0), (GW, GW))),
        jnp.pad(rwkv_g2, ((0, 0), (0, 0), (2 * GW, 0))),
        jnp.zeros((L, 256 - 160, 3 * GW), F32)], axis=1).astype(BF16)
    bias_full = jnp.repeat(jnp.swapaxes(sgu_b, 1, 2), HEAD_DIM, axis=2)
    swa_qg, swa_kg = tile_heads(swa_qn_g, GH), tile_heads(swa_kn_g, 2)
    fox_qg, fox_kg = tile_heads(fox_qn_g, GH), tile_heads(fox_kn_g, GH)
    fox_b = row(_pad_cols(fox_fb, 128))

    x2 = x.reshape(M, D)
    for l in range(L):
        shift1, scale1, gate1, shift2, scale2, gate2 = [mod[l, :, s] for s in range(6)]
        w_in_l = _permute_in_cols(w_in[l]).astype(BF16)
        proj = _in_call(x2, norm1_g[l].reshape(1, D), scale1, shift1, w_in_l, T)

        vec = lambda a: a.reshape(1, GW)
        mm, nn, rr, y0, bonus, g = _rwkv_local_call(
            proj, mu_rkv[l], mu_lr[l], wlr[l], vec(rwkv_w0[l]), vec(rwkv_a0[l]), vec(rwkv_kk[l]),
            vec(rwkv_ka[l]), vec(rwkv_rk[l]), e512, T)
        ya = _rwkv_state_call(mm, nn, rr, y0, bonus, g, vec(rwkv_lnx_g[l]), vec(rwkv_lnx_b[l]), e512, B, T)

        yb = _swa_call(proj, swa_sinks[l], swa_qg[l], swa_kg[l], e512, T)

        w_s = jnp.zeros((GH, BLOCK, BLOCK), F32).at[:, rows, cols].set(sgu_ws[l])
        ws_cat = jnp.concatenate([w_s[0::2], w_s[1::2]], axis=2).astype(BF16)
        yc = _sgu_call(proj, vec(sgu_ln_g[l]), vec(sgu_ln_b[l]), ws_cat, bias_full[l])

        qt, ka, vt = _fox_prep_call(proj, fox_qg[l], fox_kg[l], fox_b[l], e512, T)
        yd = _fox_call(proj, qt, ka, vt, T)

        x2 = _out_call(ya, yb, yc, yd, w_out[l].astype(BF16), x2, gate1, T)
        x2 = _ffn_call(x2, norm2_g[l].reshape(1, D), scale2, shift2, gate2,
                       ffn_w1[l].astype(BF16), ffn_w3[l].astype(BF16), ffn_w2[l].astype(BF16), T)
    return x2.reshape(B, T, D)
```

```python
import functools

import numpy as np
import jax
import jax.numpy as jnp
from jax import lax
from jax.experimental import pallas as pl
from jax.experimental.pallas import tpu as pltpu

F32 = jnp.float32
BF16 = jnp.bfloat16

D_MODEL = 2048
HEAD_DIM = 64
GW = 512
GH = 8
BLOCK = 128
NORM_EPS = 1e-6
LN_EPS = 1e-5
NEG_INF = -1e30
RWKV_GN_EPS = 64e-5
NORM_ROWS = 16
SWA_GROUPS = 4
FOX_SKEW = (3, 6)
RWKV_CHUNK = 64
RWKV_CHUNKS_PER_STEP = 4
RWKV_STATE_CHUNKS_PER_STEP = 2
LOG2E = float(np.log2(np.e))
FFN_HIDDEN = 5632

OFF_FQ, OFF_FK, OFF_FV, OFF_FG = 0, 512, 1024, 1536
OFF_SU, OFF_SV = 2048, 2560
OFF_RR = 3072
OFF_WQ = 4608
OFF_WKV = 5120
OFF_RLR = 5376
OFF_FF = OFF_RLR + 160
NP = 5632

VMEM_LIMIT = 56 * 1024 * 1024


def _cparams(sem):
    return pltpu.CompilerParams(dimension_semantics=sem, vmem_limit_bytes=VMEM_LIMIT)


def _dot(a, b):
    return jnp.dot(a, b, preferred_element_type=F32)


def _dot_nt(a, b):
    return lax.dot_general(a, b, (((1,), (1,)), ((), ())), preferred_element_type=F32)


def _dot_tn(a, b):
    return lax.dot_general(a, b, (((0,), (0,)), ((), ())), preferred_element_type=F32)


def _seg_mean(x, e_ref):
    return _dot(x.astype(BF16), e_ref[...])


def _split3(x):
    hi = x.astype(BF16)
    r1 = x - hi.astype(F32)
    mid = r1.astype(BF16)
    lo = (r1 - mid.astype(F32)).astype(BF16)
    return hi, mid, lo


def _half_masks(shape):
    lane = lax.broadcasted_iota(jnp.int32, shape, len(shape) - 1)
    m0 = (lane % 128) < 64
    return m0


def _mod_kernel(c_ref, w_ref, b_ref, o_ref):
    c = c_ref[...]
    ca = (c * jax.nn.sigmoid(c)).astype(BF16)
    o_ref[0] = _dot(ca, w_ref[0].astype(BF16)) + b_ref[0]


def _mod_call(c, w_mod, b_mod):
    L, D, N = w_mod.shape
    B = c.shape[0]
    tn = 1024
    return pl.pallas_call(
        _mod_kernel,
        grid=(L, N // tn),
        in_specs=[
            pl.BlockSpec((B, D), lambda l, j: (0, 0)),
            pl.BlockSpec((1, D, tn), lambda l, j: (l, 0, j)),
            pl.BlockSpec((1, 1, tn), lambda l, j: (l, 0, j)),
        ],
        out_specs=pl.BlockSpec((1, B, tn), lambda l, j: (l, 0, j)),
        out_shape=jax.ShapeDtypeStruct((L, B, N), F32),
        compiler_params=_cparams(("parallel", "parallel")),
        name="adaln_mod",
    )(c, w_mod, b_mod.reshape(L, 1, N))


def _norm_modulate(x_ref, g_ref, sc_ref, sh_ref, h_ref):
    gs = g_ref[...] * (1.0 + sc_ref[0])
    sh = sh_ref[0]

    def body(c, carry):
        rows = pl.ds(pl.multiple_of(c * NORM_ROWS, NORM_ROWS), NORM_ROWS)
        x = x_ref[rows, :]
        ms = jnp.mean(x * x, axis=-1, keepdims=True)
        h_ref[rows, :] = (x * lax.rsqrt(ms + NORM_EPS) * gs + sh).astype(BF16)
        return carry

    lax.fori_loop(0, x_ref.shape[0] // NORM_ROWS, body, 0, unroll=8)


def _in_kernel(x_ref, g_ref, sc_ref, sh_ref, w_ref, o_ref, h_ref):
    @pl.when(pl.program_id(1) == 0)
    def _():
        _norm_modulate(x_ref, g_ref, sc_ref, sh_ref, h_ref)

    o_ref[...] = _dot(h_ref[...], w_ref[...])


def _in_call(x2, g, scale, shift, w, T):
    M, D = x2.shape
    n = w.shape[1]
    tm, tn = 512, 2816
    return pl.pallas_call(
        _in_kernel,
        grid=(M // tm, n // tn),
        in_specs=[
            pl.BlockSpec((tm, D), lambda i, j: (i, 0)),
            pl.BlockSpec((1, D), lambda i, j: (0, 0)),
            pl.BlockSpec((1, 1, D), lambda i, j: (i * tm // T, 0, 0)),
            pl.BlockSpec((1, 1, D), lambda i, j: (i * tm // T, 0, 0)),
            pl.BlockSpec((D, tn), lambda i, j: (0, j)),
        ],
        out_specs=pl.BlockSpec((tm, tn), lambda i, j: (i, j)),
        out_shape=jax.ShapeDtypeStruct((M, n), F32),
        scratch_shapes=[pltpu.VMEM((tm, D), BF16)],
        compiler_params=_cparams(("parallel", "arbitrary")),
        name="norm_in_proj",
    )(x2, g, scale, shift, w)


def _out_kernel(ya_ref, yb_ref, yc_ref, yd_ref, w_ref, x_ref, gt_ref, o_ref):
    acc = _dot(ya_ref[...], w_ref[0:GW, :])
    acc += _dot(yb_ref[...], w_ref[GW:2 * GW, :])
    acc += _dot(yc_ref[...], w_ref[2 * GW:3 * GW, :])
    acc += _dot(yd_ref[...], w_ref[3 * GW:4 * GW, :])
    o_ref[...] = x_ref[...] + gt_ref[0] * acc


def _out_call(ya, yb, yc, yd, w, x2, gate, T):
    M, D = x2.shape
    tm = 512
    yspec = pl.BlockSpec((tm, GW), lambda i: (i, 0))
    return pl.pallas_call(
        _out_kernel,
        grid=(M // tm,),
        in_specs=[
            yspec, yspec, yspec, yspec,
            pl.BlockSpec((D, D), lambda i: (0, 0)),
            pl.BlockSpec((tm, D), lambda i: (i, 0)),
            pl.BlockSpec((1, 1, D), lambda i: (i * tm // T, 0, 0)),
        ],
        out_specs=pl.BlockSpec((tm, D), lambda i: (i, 0)),
        out_shape=jax.ShapeDtypeStruct((M, D), F32),
        compiler_params=_cparams(("parallel",)),
        name="out_proj_residual",
    )(ya, yb, yc, yd, w, x2, gate)


def _ffn_kernel(x_ref, g_ref, sc_ref, sh_ref, gt_ref, w1_ref, w3_ref, w2_ref, o_ref, h_ref):
    @pl.when(pl.program_id(1) == 0)
    def _():
        _norm_modulate(x_ref, g_ref, sc_ref, sh_ref, h_ref)
        o_ref[...] = x_ref[...]

    h = h_ref[...]
    half = w1_ref.shape[1] // 2
    fs = []
    for c in range(2):
        cols = slice(c * half, (c + 1) * half)
        a = _dot(h, w1_ref[:, cols])
        b = _dot(h, w3_ref[:, cols])
        fs.append((a * jax.nn.sigmoid(a) * b).astype(BF16))
    o_ref[...] += gt_ref[0] * _dot(jnp.concatenate(fs, axis=1), w2_ref[...])


def _ffn_call(x2, g, scale, shift, gate, w1, w3, w2, T):
    M, D = x2.shape
    F = w1.shape[1]
    tm, tf = 1024, 512
    bspec = pl.BlockSpec((1, 1, D), lambda i, j: (i * tm // T, 0, 0))
    return pl.pallas_call(
        _ffn_kernel,
        grid=(M // tm, F // tf),
        in_specs=[
            pl.BlockSpec((tm, D), lambda i, j: (i, 0)),
            pl.BlockSpec((1, D), lambda i, j: (0, 0)),
            bspec, bspec, bspec,
            pl.BlockSpec((D, tf), lambda i, j: (0, j)),
            pl.BlockSpec((D, tf), lambda i, j: (0, j)),
            pl.BlockSpec((tf, D), lambda i, j: (j, 0)),
        ],
        out_specs=pl.BlockSpec((tm, D), lambda i, j: (i, 0)),
        out_shape=jax.ShapeDtypeStruct((M, D), F32),
        scratch_shapes=[pltpu.VMEM((tm, D), BF16)],
        compiler_params=_cparams(("parallel", "arbitrary")),
        name="swiglu_ffn",
    )(x2, g, scale, shift, gate, w1, w3, w2)


def _gelu(x):
    return 0.5 * x * (1.0 + lax.erf(x * np.float32(1.0 / np.sqrt(2.0))))


def _sgu_kernel(u_ref, v_ref, g_ref, b_ref, ws_ref, bias_ref, o_ref, *, chunks):
    m0 = _half_masks((BLOCK, 128))
    for c in range(chunks):
        rows = slice(c * BLOCK, (c + 1) * BLOCK)
        u = _gelu(u_ref[rows, :])
        v = _gelu(v_ref[rows, :])
        mu = jnp.mean(v, axis=-1, keepdims=True)
        vc = v - mu
        var = jnp.mean(vc * vc, axis=-1, keepdims=True)
        vn = vc * lax.rsqrt(var + LN_EPS) * g_ref[...] + b_ref[...]
        outs = []
        for j in range(GH // 2):
            vp = vn[:, 128 * j:128 * (j + 1)]
            stacked = jnp.concatenate(
                [jnp.where(m0, vp, 0.0), jnp.where(m0, 0.0, vp)], axis=0).astype(BF16)
            outs.append(_dot(ws_ref[j], stacked))
        z = jnp.concatenate(outs, axis=1) + bias_ref[...]
        o_ref[rows, :] = (u * z).astype(BF16)


def _sgu_call(proj, ln_g, ln_b, ws_cat, bias_full):
    M = proj.shape[0]
    chunks = 4
    tr = chunks * BLOCK
    return pl.pallas_call(
        functools.partial(_sgu_kernel, chunks=chunks),
        grid=(M // tr,),
        in_specs=[
            pl.BlockSpec((tr, GW), lambda i: (i, OFF_SU // GW)),
            pl.BlockSpec((tr, GW), lambda i: (i, OFF_SV // GW)),
            pl.BlockSpec((1, GW), lambda i: (0, 0)),
            pl.BlockSpec((1, GW), lambda i: (0, 0)),
            pl.BlockSpec((GH // 2, BLOCK, 2 * BLOCK), lambda i: (0, 0, 0)),
            pl.BlockSpec((BLOCK, GW), lambda i: (0, 0)),
        ],
        out_specs=pl.BlockSpec((tr, GW), lambda i: (i, 0)),
        out_shape=jax.ShapeDtypeStruct((M, GW), BF16),
        compiler_params=_cparams(("parallel",)),
        name="sgu_mix",
    )(proj, proj, ln_g, ln_b, ws_cat, bias_full)


def _swa_kernel(sink_ref, q_ref, kv_ref, kvp_ref, qg_ref, kg_ref, e_ref, o_ref, *, blocks, blocks_per_seq):
    i = pl.program_id(0)
    m0q = _half_masks((BLOCK, 128))
    m0k = _half_masks((2 * BLOCK, 128))
    row = lax.broadcasted_iota(jnp.int32, (BLOCK, 2 * BLOCK), 0)
    col = lax.broadcasted_iota(jnp.int32, (BLOCK, 2 * BLOCK), 1)
    band = (col <= row + BLOCK) & (col > row)
    e128 = e_ref.at[0:128, 0:128]

    kv_all = jnp.concatenate([kvp_ref[...], kv_ref[...]], axis=0)
    for r in range(blocks):
        first = ((i * blocks + r) % blocks_per_seq) == 0
        mask = band & (col >= jnp.where(first, BLOCK, 0))
        q = q_ref[r * BLOCK:(r + 1) * BLOCK, :]
        qn = q * lax.rsqrt(_seg_mean(q * q, e_ref) + NORM_EPS) * qg_ref[...] * np.float32(HEAD_DIM ** -0.5 * LOG2E)
        kv = kv_all[r * BLOCK:(r + 2) * BLOCK, :]
        k = kv[:, 0:128]
        v = kv[:, 128:256]
        kn = k * lax.rsqrt(_seg_mean(k * k, e128) + NORM_EPS) * kg_ref[...]
        kn_sw = pltpu.roll(kn, 64, axis=1)
        v_sw = pltpu.roll(v, 64, axis=1)
        kdup = [jnp.where(m0k, kn, kn_sw).astype(BF16), jnp.where(m0k, kn_sw, kn).astype(BF16)]
        vdup = [jnp.where(m0k, v, v_sw), jnp.where(m0k, v_sw, v)]
        vmsk = [[jnp.where(m0k, vd, 0.0).astype(BF16), jnp.where(m0k, 0.0, vd).astype(BF16)] for vd in vdup]
        scores = []
        for h in range(GH):
            qp = qn[:, 128 * (h // 2):128 * (h // 2 + 1)]
            qm = (jnp.where(m0q, qp, 0.0) if h % 2 == 0 else jnp.where(m0q, 0.0, qp)).astype(BF16)
            scores.append(_dot_nt(qm, kdup[h // SWA_GROUPS]))
        probs, inv = [], []
        for h in range(GH):
            s = jnp.where(mask, scores[h], NEG_INF)
            sink = sink_ref[h] * np.float32(LOG2E)
            m = jnp.maximum(jnp.max(s, axis=-1, keepdims=True), sink)
            p = jnp.exp2(s - m)
            inv.append(1.0 / (jnp.sum(p, axis=-1, keepdims=True) + jnp.exp2(sink - m)))
            probs.append(p.astype(BF16))
        outs = []
        for j in range(GH // 2):
            g = (2 * j) // SWA_GROUPS
            outs.append(_dot(probs[2 * j], vmsk[g][0]) * inv[2 * j]
                        + _dot(probs[2 * j + 1], vmsk[g][1]) * inv[2 * j + 1])
        o_ref[r * BLOCK:(r + 1) * BLOCK, :] = jnp.concatenate(outs, axis=1).astype(BF16)


def _swa_call(proj, sinks, qg, kg, e512, T):
    M = proj.shape[0]
    blocks = 2
    tr = blocks * BLOCK
    return pl.pallas_call(
        functools.partial(_swa_kernel, blocks=blocks, blocks_per_seq=T // BLOCK),
        grid=(M // tr,),
        in_specs=[
            pl.BlockSpec(memory_space=pltpu.SMEM),
            pl.BlockSpec((tr, GW), lambda i: (i, OFF_WQ // GW)),
            pl.BlockSpec((tr, 256), lambda i: (i, OFF_WKV // 256)),
            pl.BlockSpec((BLOCK, 256), lambda i: (jnp.maximum(i * blocks - 1, 0), OFF_WKV // 256)),
            pl.BlockSpec((1, GW), lambda i: (0, 0)),
            pl.BlockSpec((1, 128), lambda i: (0, 0)),
            pl.BlockSpec((GW, GW), lambda i: (0, 0)),
        ],
        out_specs=pl.BlockSpec((tr, GW), lambda i: (i, 0)),
        out_shape=jax.ShapeDtypeStruct((M, GW), BF16),
        compiler_params=_cparams(("parallel",)),
        name="swa_mix",
    )(sinks, proj, proj, proj, qg, kg, e512)


def _fox_prep_kernel(q_ref, k_ref, v_ref, f_ref, qg_ref, kg_ref, fb_ref, e_ref, tri_ref, sel_ref, ones_ref,
                     qt_ref, ka_ref, vt_ref, carry_ref):
    @pl.when(pl.program_id(1) == 0)
    def _():
        carry_ref[...] = jnp.zeros_like(carry_ref)

    q = q_ref[...]
    k = k_ref[...]
    v = v_ref[...]
    tr = q.shape[0]
    qn = q * lax.rsqrt(_seg_mean(q * q, e_ref) + NORM_EPS) * qg_ref[...] * np.float32(HEAD_DIM ** -0.5 * LOG2E)
    kn = k * lax.rsqrt(_seg_mean(k * k, e_ref) + NORM_EPS) * kg_ref[...]
    z = f_ref[...] + fb_ref[...]
    logf2 = (jnp.minimum(z, 0.0) - jnp.log1p(jnp.exp(-jnp.abs(z)))) * np.float32(LOG2E)
    tri = tri_ref[...]
    cum = sum(_dot(tri, part) for part in _split3(logf2)) + carry_ref[...]
    carry_ref[...] = cum[tr - 1:tr, :]
    hi, mid, lo = _split3(-cum)
    bias = _dot(jnp.concatenate([hi, mid, lo], axis=1), sel_ref[...])
    m0 = _half_masks((tr, 128))
    for h in range(GH):
        p, half = divmod(h, 2)
        slot = slice(128 * h, 128 * (h + 1))
        pair = slice(128 * p, 128 * (p + 1))
        own_k, own_q, other_k, other_q = kn[:, pair], qn[:, pair], bias[:, slot], ones_ref[:, slot]
        if half == 0:
            ka = jnp.where(m0, own_k, other_k)
            qa = jnp.where(m0, own_q, other_q)
        else:
            ka = jnp.where(m0, other_k, own_k)
            qa = jnp.where(m0, other_q, own_q)
        ka_ref[:, slot] = ka.astype(BF16)
        qt_ref[0, slot, :] = qa.T.astype(BF16)
    for p in range(GH // 2):
        pair = slice(128 * p, 128 * (p + 1))
        vt_ref[0, pair, :] = v[:, pair].T.astype(BF16)


def _fox_prep_call(proj, qg, kg, fb, e512, T):
    M = proj.shape[0]
    B = M // T
    tr = 256
    nb = T // tr
    tri = jnp.tril(jnp.ones((tr, tr), BF16))
    sel = np.zeros((3 * 128, GH * 128), np.float32)
    ones = np.zeros((1, GH * 128), np.float32)
    for h in range(GH):
        off = 128 * h + (64 if h % 2 == 0 else 0)
        for part in range(3):
            sel[128 * part + OFF_FF % 128 + h, off + part] = 1.0
            ones[0, off + part] = 1.0
    cspec = lambda c: pl.BlockSpec((tr, GW), lambda b, i, c=c: (b * nb + i, c))
    return pl.pallas_call(
        _fox_prep_kernel,
        grid=(B, nb),
        in_specs=[
            cspec(OFF_FQ // GW), cspec(OFF_FK // GW), cspec(OFF_FV // GW),
            pl.BlockSpec((tr, 128), lambda b, i: (b * nb + i, OFF_FF // 128)),
            pl.BlockSpec((1, GW), lambda b, i: (0, 0)),
            pl.BlockSpec((1, GW), lambda b, i: (0, 0)),
            pl.BlockSpec((1, 128), lambda b, i: (0, 0)),
            pl.BlockSpec((GW, GW), lambda b, i: (0, 0)),
            pl.BlockSpec((tr, tr), lambda b, i: (0, 0)),
            pl.BlockSpec((3 * 128, GH * 128), lambda b, i: (0, 0)),
            pl.BlockSpec((1, GH * 128), lambda b, i: (0, 0)),
        ],
        out_specs=[
            pl.BlockSpec((1, GH * 128, tr), lambda b, i: (b, 0, i)),
            pl.BlockSpec((tr, GH * 128), lambda b, i: (b * nb + i, 0)),
            pl.BlockSpec((1, GW, tr), lambda b, i: (b, 0, i)),
        ],
        out_shape=[
            jax.ShapeDtypeStruct((B, GH * 128, T), BF16),
            jax.ShapeDtypeStruct((M, GH * 128), BF16),
            jax.ShapeDtypeStruct((B, GW, T), BF16),
        ],
        scratch_shapes=[pltpu.VMEM((1, 128), F32)],
        compiler_params=_cparams(("parallel", "arbitrary")),
        name="fox_prep",
    )(proj, proj, proj, proj, qg, kg, fb, e512, tri, jnp.asarray(sel, BF16), jnp.asarray(ones, F32))


def _fox_kernel(qt_ref, ka_ref, vt_ref, gl_ref, o_ref, m_ref, l_ref, acc_ref, *, tq):
    i = pl.program_id(1)
    m_ref[...] = jnp.full(m_ref.shape, NEG_INF, F32)
    l_ref[...] = jnp.zeros(l_ref.shape, F32)
    acc_ref[...] = jnp.zeros(acc_ref.shape, F32)

    def step(start, nk, masked):
        start = pl.multiple_of(start, tq)
        if masked:
            key = lax.broadcasted_iota(jnp.int32, (nk, tq), 0) + start
            qry = lax.broadcasted_iota(jnp.int32, (nk, tq), 1) + i * tq
            keep = key <= qry
        sts, pts, alphas = {}, {}, {}
        d1, d2 = FOX_SKEW
        for t in range(GH + d2):
            if t < GH:
                slot = slice(128 * t, 128 * (t + 1))
                sts[t] = _dot(ka_ref[pl.ds(start, nk), slot], qt_ref[0, slot, :])
            if d1 <= t < GH + d1:
                h = t - d1
                st = jnp.where(keep, sts.pop(h), NEG_INF) if masked else sts.pop(h)
                m_old = m_ref[h]
                m_new = jnp.maximum(m_old, jnp.max(st, axis=0, keepdims=True))
                alphas[h] = jnp.exp2(m_old - m_new)
                pt = jnp.exp2(st - m_new)
                l_ref[h] = alphas[h] * l_ref[h] + jnp.sum(pt, axis=0, keepdims=True)
                m_ref[h] = m_new
                pts[h] = pt.astype(BF16)
            if t >= d2:
                h = t - d2
                vth = vt_ref[0, 64 * h:64 * (h + 1), pl.ds(start, nk)]
                acc_ref[h] = alphas.pop(h) * acc_ref[h] + _dot(vth, pts.pop(h))

    def body(kb, carry):
        step(kb * (2 * tq), 2 * tq, False)
        return carry

    lax.fori_loop(0, i // 2, body, 0)

    @pl.when(i % 2 == 1)
    def _():
        step((i - 1) * tq, 2 * tq, True)

    @pl.when(i % 2 == 0)
    def _():
        step(i * tq, tq, True)

    for p in range(GH // 2):
        pair = slice(128 * p, 128 * (p + 1))
        ot = jnp.concatenate([acc_ref[2 * p] / l_ref[2 * p], acc_ref[2 * p + 1] / l_ref[2 * p + 1]], axis=0)
        o_ref[:, pair] = (ot.T * jax.nn.sigmoid(gl_ref[:, pair])).astype(BF16)


def _fox_call(proj, qt, ka, vt, T):
    M = proj.shape[0]
    B = M // T
    tq = 256
    nq = T // tq
    return pl.pallas_call(
        functools.partial(_fox_kernel, tq=tq),
        grid=(B, nq),
        in_specs=[
            pl.BlockSpec((1, GH * 128, tq), lambda b, i: (b, 0, i)),
            pl.BlockSpec((T, GH * 128), lambda b, i: (b, 0)),
            pl.BlockSpec((1, GW, T), lambda b, i: (b, 0, 0)),
            pl.BlockSpec((tq, GW), lambda b, i: (b * nq + i, OFF_FG // GW)),
        ],
        out_specs=pl.BlockSpec((tq, GW), lambda b, i: (b * nq + i, 0)),
        out_shape=jax.ShapeDtypeStruct((M, GW), BF16),
        scratch_shapes=[
            pltpu.VMEM((GH, 1, tq), F32),
            pltpu.VMEM((GH, 1, tq), F32),
            pltpu.VMEM((GH, HEAD_DIM, tq), F32),
        ],
        compiler_params=_cparams(("parallel", "arbitrary")),
        name="fox_attn",
    )(qt, ka, vt, proj)


def _rwkv_local_kernel(p_ref, lr_ref, pp_ref, lrp_ref, mu_ref, mulr_ref, wlr_ref, w0_ref, a0_ref,
                       kk_ref, ka_ref, rk_ref, e_ref, tri_ref,
                       m_ref, n_ref, r_ref, y0_ref, bonus_ref, g_ref, *, chunks_per_seq, nc):
    C = RWKV_CHUNK
    i = pl.program_id(0)
    has_prev = (((i * nc) % chunks_per_seq) != 0).astype(F32)
    row = lax.broadcasted_iota(jnp.int32, (nc * C, 1), 0)

    def shifted(cur_ref, prev_ref, mu):
        cur = cur_ref[...]
        prev = jnp.where(row == 0, prev_ref[7:8, :] * has_prev, pltpu.roll(cur, 1, axis=0))
        return cur + (prev - cur) * mu

    p = shifted(p_ref, pp_ref, mu_ref[...])
    lr = shifted(lr_ref, lrp_ref, mulr_ref[...])
    r = p[:, 0:GW]
    k = p[:, GW:2 * GW]
    v = p[:, 2 * GW:3 * GW]

    lane = lax.broadcasted_iota(jnp.int32, lr.shape, 1)
    z = jnp.where(lane < 32, jnp.tanh(lr), jnp.where(lane < 64, lr, jax.nn.sigmoid(lr)))
    low = _dot(z.astype(BF16), wlr_ref[...])
    lw = -np.float32(np.exp(-0.5)) * jax.nn.sigmoid(w0_ref[...] + low[:, 0:GW])
    a = jax.nn.sigmoid(a0_ref[...] + low[:, GW:2 * GW])
    g_ref[...] = low[:, 2 * GW:3 * GW]

    kk = k * kk_ref[...]
    n2 = _seg_mean(kk * kk, e_ref) * np.float32(HEAD_DIM)
    kk = kk * lax.rsqrt(jnp.maximum(n2, 1e-24))
    k = k * (1.0 + (a - 1.0) * ka_ref[...])
    b = kk * a
    bonus_ref[...] = _seg_mean(r * k * rk_ref[...], e_ref) * np.float32(HEAD_DIM) * v

    tri = tri_ref[...]
    lw_hi = lw.astype(BF16)
    lw_lo = (lw - lw_hi.astype(F32)).astype(BF16)
    L_all = _dot(tri, lw_hi) + _dot(tri, lw_lo)

    m0 = _half_masks((C, 128))
    ri = lax.broadcasted_iota(jnp.int32, (2 * C, 2 * C), 0)
    ci = lax.broadcasted_iota(jnp.int32, (2 * C, 2 * C), 1)
    strict = (ri % C) > (ci % C)
    incl = (ri % C) >= (ci % C)
    eye = ri == ci

    def stack(xp):
        return jnp.concatenate([jnp.where(m0, xp, 0.0), jnp.where(m0, 0.0, xp)], axis=0)

    chains = [(c, j) for c in range(nc) for j in range(GH // 2)]
    st = []
    for c, j in chains:
        rs = slice(c * C, (c + 1) * C)
        sl = slice(128 * j, 128 * (j + 1))
        L = L_all[rs, sl]
        Lend = L[C - 1:C, :]
        e_pos = jnp.exp(L)
        e_neg = jnp.exp(-L)
        e_hat = jnp.exp(Lend - L)
        rc, kc, bc = r[rs, sl], k[rs, sl], b[rs, sl]
        rst = stack(rc * e_pos)
        ast_b = stack(-kk[rs, sl] * jnp.exp(L - lw[rs, sl])).astype(BF16)
        st.append(dict(
            rs=rs, sl=sl, rst=rst, ast_b=ast_b, vst_b=stack(v[rs, sl]).astype(BF16),
            left=jnp.concatenate([ast_b, rst.astype(BF16)], axis=0),
            right=jnp.concatenate([stack(bc * e_neg), stack(kc * e_neg)], axis=0).astype(BF16),
            bkh=jnp.concatenate([stack(bc * e_hat), stack(kc * e_hat)], axis=0).astype(BF16),
            wc=jnp.exp(Lend)))
    for s in st:
        G = _dot_nt(s["left"], s["right"])
        a_ab = jnp.where(strict, G[0:2 * C, 0:2 * C], 0.0)
        s["a_ak"] = jnp.where(strict, G[0:2 * C, 2 * C:4 * C], 0.0).astype(BF16)
        s["a_r"] = jnp.concatenate([jnp.where(incl, G[2 * C:4 * C, 0:2 * C], 0.0),
                                    jnp.where(incl, G[2 * C:4 * C, 2 * C:4 * C], 0.0)], axis=1).astype(BF16)
        s["t_inv"] = jnp.where(eye, 1.0, a_ab)
        s["apow"] = a_ab.astype(BF16)
    for s in st:
        s["akv"] = _dot(s["a_ak"], s["vst_b"]).astype(BF16)
    for s in st:
        s["apow"] = _dot(s["apow"], s["apow"]).astype(BF16)
    for _ in range(4):
        for s in st:
            res = _dot(s["apow"], jnp.concatenate([s["apow"], s["t_inv"].astype(BF16)], axis=1))
            s["apow"] = res[:, 0:2 * C].astype(BF16)
            s["t_inv"] = s["t_inv"] + res[:, 2 * C:4 * C]
    for s in st:
        s["t_inv"] = s["t_inv"] + _dot(s["apow"], s["t_inv"].astype(BF16))
    for s in st:
        pq = _dot(s["t_inv"].astype(BF16), jnp.concatenate([s["ast_b"], s["akv"]], axis=1))
        s["rhs2"] = jnp.concatenate(
            [pq.astype(BF16), jnp.concatenate([jnp.zeros_like(s["vst_b"]), s["vst_b"]], axis=1)], axis=0)
    for s in st:
        ry = _dot(s["a_r"], s["rhs2"])
        rhat = s["rst"] + ry[:, 0:128]
        y0 = ry[:, 128:256]
        r_ref[s["rs"], s["sl"]] = (rhat[0:C, :] + rhat[C:2 * C, :]).astype(BF16)
        y0_ref[s["rs"], s["sl"]] = y0[0:C, :] + y0[C:2 * C, :]
    for (c, j), s in zip(chains, st):
        mn = _dot_tn(s["bkh"], s["rhs2"])
        m_ref[c, j] = (mn[:, 0:128] + jnp.where(eye, s["wc"], 0.0)).astype(BF16)
        n_ref[c, j] = mn[:, 128:256]


def _rwkv_local_call(proj, mu_rkv, mu_lr, wlr, w0, a0, kk, ka, rk, e512, T):
    M = proj.shape[0]
    C = RWKV_CHUNK
    nc = RWKV_CHUNKS_PER_STEP
    R = nc * C
    nchunks = M // C
    ridx = jnp.arange(R)
    tri = ((ridx[:, None] >= ridx[None, :]) & (ridx[:, None] // C == ridx[None, :] // C)).astype(BF16)
    vec = lambda n: pl.BlockSpec((1, n), lambda i: (0, 0))
    rowspec = pl.BlockSpec((R, GW), lambda i: (i, 0))
    matspec = pl.BlockSpec((nc, GH // 2, 128, 128), lambda i: (i, 0, 0, 0))
    prev = lambda i: jnp.maximum(i * (R // 8) - 1, 0)
    return pl.pallas_call(
        functools.partial(_rwkv_local_kernel, chunks_per_seq=T // C, nc=nc),
        grid=(nchunks // nc,),
        in_specs=[
            pl.BlockSpec((R, 3 * GW), lambda i: (i, OFF_RR // (3 * GW))),
            pl.BlockSpec((R, 256), lambda i: (i, OFF_RLR // 256)),
            pl.BlockSpec((8, 3 * GW), lambda i: (prev(i), OFF_RR // (3 * GW))),
            pl.BlockSpec((8, 256), lambda i: (prev(i), OFF_RLR // 256)),
            vec(3 * GW), vec(256),
            pl.BlockSpec((256, 3 * GW), lambda i: (0, 0)),
            vec(GW), vec(GW), vec(GW), vec(GW), vec(GW),
            pl.BlockSpec((GW, GW), lambda i: (0, 0)),
            pl.BlockSpec((R, R), lambda i: (0, 0)),
        ],
        out_specs=[matspec, matspec, rowspec, rowspec, rowspec, rowspec],
        out_shape=[
            jax.ShapeDtypeStruct((nchunks, GH // 2, 128, 128), BF16),
            jax.ShapeDtypeStruct((nchunks, GH // 2, 128, 128), F32),
            jax.ShapeDtypeStruct((M, GW), BF16),
            jax.ShapeDtypeStruct((M, GW), F32),
            jax.ShapeDtypeStruct((M, GW), F32),
            jax.ShapeDtypeStruct((M, GW), F32),
        ],
        compiler_params=_cparams(("parallel",)),
        name="rwkv_chunk_local",
    )(proj, proj, proj, proj, mu_rkv, mu_lr, wlr, w0, a0, kk, ka, rk, e512, tri)


def _rwkv_state_kernel(m_ref, n_ref, r_ref, y0_ref, bonus_ref, g_ref, lg_ref, lb_ref, e_ref,
                       o_ref, h_ref, *, batch, steps):
    @pl.when(pl.program_id(0) == 0)
    def _():
        h_ref[...] = jnp.zeros_like(h_ref)

    C = RWKV_CHUNK
    for c in range(steps):
        rows = slice(c * C, (c + 1) * C)
        ys = {}
        for b in range(batch):
            for j in range(GH // 2):
                sl = slice(128 * j, 128 * (j + 1))
                h = h_ref[b, j].astype(BF16)
                ys[b, j] = _dot(r_ref[b, rows, sl], h) + y0_ref[b, rows, sl]
                h_ref[b, j] = _dot(m_ref[b, c, j], h) + n_ref[b, c, j]
        for b in range(batch):
            y = jnp.concatenate([ys[b, j] for j in range(GH // 2)], axis=1)
            mean = _seg_mean(y, e_ref)
            yc = y - mean
            var = _seg_mean(yc * yc, e_ref)
            yn = yc * lax.rsqrt(var + RWKV_GN_EPS) * lg_ref[...] + lb_ref[...]
            o_ref[b, rows, :] = ((yn + bonus_ref[b, rows, :]) * g_ref[b, rows, :]).astype(BF16)


def _rwkv_state_call(mm, nn, rr, y0, bonus, g, lnx_g, lnx_b, e512, B, T):
    C = RWKV_CHUNK
    steps = RWKV_STATE_CHUNKS_PER_STEP
    nc = T // C
    mm = mm.reshape(B, nc, GH // 2, 128, 128)
    nn = nn.reshape(B, nc, GH // 2, 128, 128)
    r3 = lambda a: a.reshape(B, T, GW)
    matspec = pl.BlockSpec((B, steps, GH // 2, 128, 128), lambda c: (0, c, 0, 0, 0))
    rowspec = pl.BlockSpec((B, steps * C, GW), lambda c: (0, c, 0))
    vec = pl.BlockSpec((1, GW), lambda c: (0, 0))
    out = pl.pallas_call(
        functools.partial(_rwkv_state_kernel, batch=B, steps=steps),
        grid=(nc // steps,),
        in_specs=[matspec, matspec, rowspec, rowspec, rowspec, rowspec, vec, vec,
                  pl.BlockSpec((GW, GW), lambda c: (0, 0))],
        out_specs=rowspec,
        out_shape=jax.ShapeDtypeStruct((B, T, GW), BF16),
        scratch_shapes=[pltpu.VMEM((B, GH // 2, 128, 128), F32)],
        compiler_params=_cparams(("arbitrary",)),
        name="rwkv_state_scan",
    )(mm, nn, r3(rr), r3(y0), r3(bonus), r3(g), lnx_g, lnx_b, e512)
    return out.reshape(B * T, GW)


def _pad_cols(a, n):
    return jnp.pad(a, [(0, 0)] * (a.ndim - 1) + [(0, n - a.shape[-1])])


def _permute_in_cols(w):
    rw, sw, sg, fx = 0, 1696, 2464, 3488
    parts = [
        w[..., fx:fx + 1536],
        w[..., fx + 1544:fx + 2056],
        w[..., sg:sg + 1024],
        w[..., rw:rw + 1536],
        w[..., sw:sw + 768],
        _pad_cols(jnp.concatenate([w[..., rw + 1536:rw + 1696],
                                   w[..., fx + 1536:fx + 1544]], axis=-1), 256),
    ]
    return jnp.concatenate(parts, axis=-1)


def _tile_heads(g):
    return jnp.tile(g, GH).reshape(1, GW)


def kernel(x, c, w_mod, b_mod, norm1_g, norm2_g, w_in, w_out, rwkv_mu, rwkv_w0, rwkv_w2, rwkv_a0, rwkv_a2, rwkv_g2, rwkv_kk, rwkv_ka, rwkv_rk, rwkv_lnx_g, rwkv_lnx_b, swa_qn_g, swa_kn_g, swa_sinks, sgu_ln_g, sgu_ln_b, sgu_ws, sgu_b, fox_qn_g, fox_kn_g, fox_fb, ffn_w1, ffn_w3, ffn_w2):
    B, T, D = x.shape
    L = w_mod.shape[0]
    M = B * T

    mod = _mod_call(c, w_mod, b_mod)
    mod = mod.reshape(L, B, 6, 1, D)

    seg = jnp.arange(GW) // HEAD_DIM
    e512 = jnp.where(seg[:, None] == seg[None, :], 1.0 / HEAD_DIM, 0.0).astype(BF16)
    rows, cols = np.tril_indices(BLOCK)

    row = lambda a: a.reshape(L, 1, -1)
    tile_heads = lambda g, n: jnp.tile(g, (1, n)).reshape(L, 1, n * HEAD_DIM)
    mu_rkv = row(rwkv_mu[:, 0:3 * GW])
    mu_lr = row(_pad_cols(rwkv_mu[:, 3 * GW:], 256))
    wlr = jnp.concatenate([
        jnp.pad(rwkv_w2, ((0, 0), (0, 0), (0, 2 * GW))),
        jnp.pad(rwkv_a2, ((0, 0), (0, 0), (GW, GW))),
        jnp.pad(rwkv_g2, ((0, 0), (0, 0), (2 * GW, 0))),
        jnp.zeros((L, 256 - 160, 3 * GW), F32)], axis=1).astype(BF16)
    bias_full = jnp.repeat(jnp.swapaxes(sgu_b, 1, 2), HEAD_DIM, axis=2)
    swa_qg, swa_kg = tile_heads(swa_qn_g, GH), tile_heads(swa_kn_g, 2)
    fox_qg, fox_kg = tile_heads(fox_qn_g, GH), tile_heads(fox_kn_g, GH)
    fox_b = row(jnp.pad(fox_fb, ((0, 0), (OFF_FF % 128, 128 - GH - OFF_FF % 128))))

    x2 = x.reshape(M, D)
    for l in range(L):
        shift1, scale1, gate1, shift2, scale2, gate2 = [mod[l, :, s] for s in range(6)]
        w_in_l = _permute_in_cols(w_in[l]).astype(BF16)
        proj = _in_call(x2, norm1_g[l].reshape(1, D), scale1, shift1, w_in_l, T)

        vec = lambda a: a.reshape(1, GW)
        mm, nn, rr, y0, bonus, g = _rwkv_local_call(
            proj, mu_rkv[l], mu_lr[l], wlr[l], vec(rwkv_w0[l]), vec(rwkv_a0[l]), vec(rwkv_kk[l]),
            vec(rwkv_ka[l]), vec(rwkv_rk[l]), e512, T)
        ya = _rwkv_state_call(mm, nn, rr, y0, bonus, g, vec(rwkv_lnx_g[l]), vec(rwkv_lnx_b[l]), e512, B, T)

        yb = _swa_call(proj, swa_sinks[l], swa_qg[l], swa_kg[l], e512, T)

        w_s = jnp.zeros((GH, BLOCK, BLOCK), F32).at[:, rows, cols].set(sgu_ws[l])
        ws_cat = jnp.concatenate([w_s[0::2], w_s[1::2]], axis=2).astype(BF16)
        yc = _sgu_call(proj, vec(sgu_ln_g[l]), vec(sgu_ln_b[l]), ws_cat, bias_full[l])

        qt, ka, vt = _fox_prep_call(proj, fox_qg[l], fox_kg[l], fox_b[l], e512, T)
        yd = _fox_call(proj, qt, ka, vt, T)

        x2 = _out_call(ya, yb, yc, yd, w_out[l].astype(BF16), x2, gate1, T)
        x2 = _ffn_call(x2, norm2_g[l].reshape(1, D), scale2, shift2, gate2,
                       ffn_w1[l].astype(BF16), ffn_w3[l].astype(BF16), ffn_w2[l].astype(BF16), T)
    return x2.reshape(B, T, D)
```

```python
import functools

import numpy as np
import jax
import jax.numpy as jnp
from jax import lax
from jax.experimental import pallas as pl
from jax.experimental.pallas import tpu as pltpu

F32 = jnp.float32
BF16 = jnp.bfloat16

D_MODEL = 2048
HEAD_DIM = 64
GW = 512
GH = 8
BLOCK = 128
NORM_EPS = 1e-6
LN_EPS = 1e-5
NEG_INF = -1e30
RWKV_GN_EPS = 64e-5
NORM_ROWS = 16
PREV_ROWS = 16
SWA_GROUPS = 4
FOX_SKEW = (3, 6)
RWKV_CHUNK = 64
RWKV_CHUNKS_PER_STEP = 4
RWKV_STATE_CHUNKS_PER_STEP = 2
LOG2E = float(np.log2(np.e))
FFN_HIDDEN = 5632

OFF_FQ, OFF_FK, OFF_FV, OFF_FG = 0, 512, 1024, 1536
OFF_SU, OFF_SV = 2048, 2560
OFF_RR = 3072
OFF_WQ = 4608
OFF_WKV = 5120
OFF_RLR = 5376
OFF_FF = OFF_RLR + 160
NP = 5632

VMEM_LIMIT = 56 * 1024 * 1024
IN_PROJ_VMEM_LIMIT = 60 * 1024 * 1024


def _cparams(sem, vmem_limit=VMEM_LIMIT):
    return pltpu.CompilerParams(dimension_semantics=sem, vmem_limit_bytes=vmem_limit)


def _dot(a, b):
    return jnp.dot(a, b, preferred_element_type=F32)


def _dot_nt(a, b):
    return lax.dot_general(a, b, (((1,), (1,)), ((), ())), preferred_element_type=F32)


def _dot_tn(a, b):
    return lax.dot_general(a, b, (((0,), (0,)), ((), ())), preferred_element_type=F32)


def _seg_mean(x, e_ref):
    return _dot(x.astype(BF16), e_ref[...])


def _split3(x):
    hi = x.astype(BF16)
    r1 = x - hi.astype(F32)
    mid = r1.astype(BF16)
    lo = (r1 - mid.astype(F32)).astype(BF16)
    return hi, mid, lo


def _half_masks(shape):
    lane = lax.broadcasted_iota(jnp.int32, shape, len(shape) - 1)
    m0 = (lane % 128) < 64
    return m0


def _mod_kernel(c_ref, w_ref, b_ref, o_ref):
    c = c_ref[...]
    ca = (c * jax.nn.sigmoid(c)).astype(BF16)
    o_ref[0] = _dot(ca, w_ref[0].astype(BF16)) + b_ref[0]


def _mod_call(c, w_mod, b_mod):
    L, D, N = w_mod.shape
    B = c.shape[0]
    tn = 1024
    return pl.pallas_call(
        _mod_kernel,
        grid=(L, N // tn),
        in_specs=[
            pl.BlockSpec((B, D), lambda l, j: (0, 0)),
            pl.BlockSpec((1, D, tn), lambda l, j: (l, 0, j)),
            pl.BlockSpec((1, 1, tn), lambda l, j: (l, 0, j)),
        ],
        out_specs=pl.BlockSpec((1, B, tn), lambda l, j: (l, 0, j)),
        out_shape=jax.ShapeDtypeStruct((L, B, N), F32),
        compiler_params=_cparams(("parallel", "parallel")),
        name="adaln_mod",
    )(c, w_mod, b_mod.reshape(L, 1, N))


def _norm_modulate(x_ref, g_ref, sc_ref, sh_ref, h_ref):
    gs = g_ref[...] * (1.0 + sc_ref[0])
    sh = sh_ref[0]

    def body(c, carry):
        rows = pl.ds(pl.multiple_of(c * NORM_ROWS, NORM_ROWS), NORM_ROWS)
        x = x_ref[rows, :]
        ms = jnp.mean(x * x, axis=-1, keepdims=True)
        h_ref[rows, :] = (x * lax.rsqrt(ms + NORM_EPS) * gs + sh).astype(BF16)
        return carry

    lax.fori_loop(0, x_ref.shape[0] // NORM_ROWS, body, 0, unroll=8)


def _in_kernel(x_ref, g_ref, sc_ref, sh_ref, w_ref, o_ref, h_ref):
    @pl.when(pl.program_id(1) == 0)
    def _():
        _norm_modulate(x_ref, g_ref, sc_ref, sh_ref, h_ref)

    o_ref[...] = _dot(h_ref[...], w_ref[...]).astype(o_ref.dtype)


def _in_call(x2, g, scale, shift, w, T):
    M, D = x2.shape
    n = w.shape[1]
    tm, tn = 1024, 2816
    return pl.pallas_call(
        _in_kernel,
        grid=(M // tm, n // tn),
        in_specs=[
            pl.BlockSpec((tm, D), lambda i, j: (i, 0)),
            pl.BlockSpec((1, D), lambda i, j: (0, 0)),
            pl.BlockSpec((1, 1, D), lambda i, j: (i * tm // T, 0, 0)),
            pl.BlockSpec((1, 1, D), lambda i, j: (i * tm // T, 0, 0)),
            pl.BlockSpec((D, tn), lambda i, j: (0, j)),
        ],
        out_specs=pl.BlockSpec((tm, tn), lambda i, j: (i, j)),
        out_shape=jax.ShapeDtypeStruct((M, n), BF16),
        scratch_shapes=[pltpu.VMEM((tm, D), BF16)],
        compiler_params=_cparams(("parallel", "arbitrary"), IN_PROJ_VMEM_LIMIT),
        name="norm_in_proj",
    )(x2, g, scale, shift, w)


def _out_kernel(ya_ref, yb_ref, yc_ref, yd_ref, w_ref, x_ref, gt_ref, o_ref):
    acc = _dot(ya_ref[...], w_ref[0:GW, :])
    acc += _dot(yb_ref[...], w_ref[GW:2 * GW, :])
    acc += _dot(yc_ref[...], w_ref[2 * GW:3 * GW, :])
    acc += _dot(yd_ref[...], w_ref[3 * GW:4 * GW, :])
    o_ref[...] = x_ref[...] + gt_ref[0] * acc


def _out_call(ya, yb, yc, yd, w, x2, gate, T):
    M, D = x2.shape
    tm = 512
    yspec = pl.BlockSpec((tm, GW), lambda i: (i, 0))
    return pl.pallas_call(
        _out_kernel,
        grid=(M // tm,),
        in_specs=[
            yspec, yspec, yspec, yspec,
            pl.BlockSpec((D, D), lambda i: (0, 0)),
            pl.BlockSpec((tm, D), lambda i: (i, 0)),
            pl.BlockSpec((1, 1, D), lambda i: (i * tm // T, 0, 0)),
        ],
        out_specs=pl.BlockSpec((tm, D), lambda i: (i, 0)),
        out_shape=jax.ShapeDtypeStruct((M, D), F32),
        compiler_params=_cparams(("parallel",)),
        name="out_proj_residual",
    )(ya, yb, yc, yd, w, x2, gate)


def _ffn_kernel(x_ref, g_ref, sc_ref, sh_ref, gt_ref, w1_ref, w3_ref, w2_ref, o_ref, h_ref):
    @pl.when(pl.program_id(1) == 0)
    def _():
        _norm_modulate(x_ref, g_ref, sc_ref, sh_ref, h_ref)
        o_ref[...] = x_ref[...]

    h = h_ref[...]
    half = w1_ref.shape[1] // 2
    fs = []
    for c in range(2):
        cols = slice(c * half, (c + 1) * half)
        a = _dot(h, w1_ref[:, cols])
        b = _dot(h, w3_ref[:, cols])
        fs.append((a * jax.nn.sigmoid(a) * b).astype(BF16))
    o_ref[...] += gt_ref[0] * _dot(jnp.concatenate(fs, axis=1), w2_ref[...])


def _ffn_call(x2, g, scale, shift, gate, w1, w3, w2, T):
    M, D = x2.shape
    F = w1.shape[1]
    tm, tf = 1024, 512
    bspec = pl.BlockSpec((1, 1, D), lambda i, j: (i * tm // T, 0, 0))
    return pl.pallas_call(
        _ffn_kernel,
        grid=(M // tm, F // tf),
        in_specs=[
            pl.BlockSpec((tm, D), lambda i, j: (i, 0)),
            pl.BlockSpec((1, D), lambda i, j: (0, 0)),
            bspec, bspec, bspec,
            pl.BlockSpec((D, tf), lambda i, j: (0, j)),
            pl.BlockSpec((D, tf), lambda i, j: (0, j)),
            pl.BlockSpec((tf, D), lambda i, j: (j, 0)),
        ],
        out_specs=pl.BlockSpec((tm, D), lambda i, j: (i, 0)),
        out_shape=jax.ShapeDtypeStruct((M, D), F32),
        scratch_shapes=[pltpu.VMEM((tm, D), BF16)],
        compiler_params=_cparams(("parallel", "arbitrary")),
        name="swiglu_ffn",
    )(x2, g, scale, shift, gate, w1, w3, w2)


def _gelu(x):
    return 0.5 * x * (1.0 + lax.erf(x * np.float32(1.0 / np.sqrt(2.0))))


def _sgu_kernel(u_ref, v_ref, g_ref, b_ref, ws_ref, bias_ref, o_ref, *, chunks):
    m0 = _half_masks((BLOCK, 128))
    for c in range(chunks):
        rows = slice(c * BLOCK, (c + 1) * BLOCK)
        u = _gelu(u_ref[rows, :].astype(F32))
        v = _gelu(v_ref[rows, :].astype(F32))
        mu = jnp.mean(v, axis=-1, keepdims=True)
        vc = v - mu
        var = jnp.mean(vc * vc, axis=-1, keepdims=True)
        vn = vc * lax.rsqrt(var + LN_EPS) * g_ref[...] + b_ref[...]
        outs = []
        for j in range(GH // 2):
            vp = vn[:, 128 * j:128 * (j + 1)]
            stacked = jnp.concatenate(
                [jnp.where(m0, vp, 0.0), jnp.where(m0, 0.0, vp)], axis=0).astype(BF16)
            outs.append(_dot(ws_ref[j], stacked))
        z = jnp.concatenate(outs, axis=1) + bias_ref[...]
        o_ref[rows, :] = (u * z).astype(BF16)


def _sgu_call(proj, ln_g, ln_b, ws_cat, bias_full):
    M = proj.shape[0]
    chunks = 4
    tr = chunks * BLOCK
    return pl.pallas_call(
        functools.partial(_sgu_kernel, chunks=chunks),
        grid=(M // tr,),
        in_specs=[
            pl.BlockSpec((tr, GW), lambda i: (i, OFF_SU // GW)),
            pl.BlockSpec((tr, GW), lambda i: (i, OFF_SV // GW)),
            pl.BlockSpec((1, GW), lambda i: (0, 0)),
            pl.BlockSpec((1, GW), lambda i: (0, 0)),
            pl.BlockSpec((GH // 2, BLOCK, 2 * BLOCK), lambda i: (0, 0, 0)),
            pl.BlockSpec((BLOCK, GW), lambda i: (0, 0)),
        ],
        out_specs=pl.BlockSpec((tr, GW), lambda i: (i, 0)),
        out_shape=jax.ShapeDtypeStruct((M, GW), BF16),
        compiler_params=_cparams(("parallel",)),
        name="sgu_mix",
    )(proj, proj, ln_g, ln_b, ws_cat, bias_full)


def _swa_kernel(sink_ref, q_ref, kv_ref, kvp_ref, qg_ref, kg_ref, e_ref, o_ref, *, blocks, blocks_per_seq):
    i = pl.program_id(0)
    m0q = _half_masks((BLOCK, 128))
    m0k = _half_masks((2 * BLOCK, 128))
    row = lax.broadcasted_iota(jnp.int32, (BLOCK, 2 * BLOCK), 0)
    col = lax.broadcasted_iota(jnp.int32, (BLOCK, 2 * BLOCK), 1)
    band = (col <= row + BLOCK) & (col > row)
    e128 = e_ref.at[0:128, 0:128]

    kv_all = jnp.concatenate([kvp_ref[...], kv_ref[...]], axis=0).astype(F32)
    for r in range(blocks):
        first = ((i * blocks + r) % blocks_per_seq) == 0
        mask = band & (col >= jnp.where(first, BLOCK, 0))
        q = q_ref[r * BLOCK:(r + 1) * BLOCK, :].astype(F32)
        qn = q * lax.rsqrt(_seg_mean(q * q, e_ref) + NORM_EPS) * qg_ref[...] * np.float32(HEAD_DIM ** -0.5 * LOG2E)
        kv = kv_all[r * BLOCK:(r + 2) * BLOCK, :]
        k = kv[:, 0:128]
        v = kv[:, 128:256]
        kn = k * lax.rsqrt(_seg_mean(k * k, e128) + NORM_EPS) * kg_ref[...]
        kn_sw = pltpu.roll(kn, 64, axis=1)
        v_sw = pltpu.roll(v, 64, axis=1)
        kdup = [jnp.where(m0k, kn, kn_sw).astype(BF16), jnp.where(m0k, kn_sw, kn).astype(BF16)]
        vdup = [jnp.where(m0k, v, v_sw), jnp.where(m0k, v_sw, v)]
        vmsk = [[jnp.where(m0k, vd, 0.0).astype(BF16), jnp.where(m0k, 0.0, vd).astype(BF16)] for vd in vdup]
        scores = []
        for h in range(GH):
            qp = qn[:, 128 * (h // 2):128 * (h // 2 + 1)]
            qm = (jnp.where(m0q, qp, 0.0) if h % 2 == 0 else jnp.where(m0q, 0.0, qp)).astype(BF16)
            scores.append(_dot_nt(qm, kdup[h // SWA_GROUPS]))
        probs, inv = [], []
        for h in range(GH):
            s = jnp.where(mask, scores[h], NEG_INF)
            sink = sink_ref[h] * np.float32(LOG2E)
            m = jnp.maximum(jnp.max(s, axis=-1, keepdims=True), sink)
            p = jnp.exp2(s - m)
            inv.append(1.0 / (jnp.sum(p, axis=-1, keepdims=True) + jnp.exp2(sink - m)))
            probs.append(p.astype(BF16))
        outs = []
        for j in range(GH // 2):
            g = (2 * j) // SWA_GROUPS
            outs.append(_dot(probs[2 * j], vmsk[g][0]) * inv[2 * j]
                        + _dot(probs[2 * j + 1], vmsk[g][1]) * inv[2 * j + 1])
        o_ref[r * BLOCK:(r + 1) * BLOCK, :] = jnp.concatenate(outs, axis=1).astype(BF16)


def _swa_call(proj, sinks, qg, kg, e512, T):
    M = proj.shape[0]
    blocks = 2
    tr = blocks * BLOCK
    return pl.pallas_call(
        functools.partial(_swa_kernel, blocks=blocks, blocks_per_seq=T // BLOCK),
        grid=(M // tr,),
        in_specs=[
            pl.BlockSpec(memory_space=pltpu.SMEM),
            pl.BlockSpec((tr, GW), lambda i: (i, OFF_WQ // GW)),
            pl.BlockSpec((tr, 256), lambda i: (i, OFF_WKV // 256)),
            pl.BlockSpec((BLOCK, 256), lambda i: (jnp.maximum(i * blocks - 1, 0), OFF_WKV // 256)),
            pl.BlockSpec((1, GW), lambda i: (0, 0)),
            pl.BlockSpec((1, 128), lambda i: (0, 0)),
            pl.BlockSpec((GW, GW), lambda i: (0, 0)),
        ],
        out_specs=pl.BlockSpec((tr, GW), lambda i: (i, 0)),
        out_shape=jax.ShapeDtypeStruct((M, GW), BF16),
        compiler_params=_cparams(("parallel",)),
        name="swa_mix",
    )(sinks, proj, proj, proj, qg, kg, e512)


def _fox_prep_kernel(q_ref, k_ref, v_ref, f_ref, qg_ref, kg_ref, fb_ref, e_ref, tri_ref, sel_ref, ones_ref,
                     qt_ref, ka_ref, vt_ref, carry_ref):
    @pl.when(pl.program_id(1) == 0)
    def _():
        carry_ref[...] = jnp.zeros_like(carry_ref)

    q = q_ref[...].astype(F32)
    k = k_ref[...].astype(F32)
    v = v_ref[...].astype(F32)
    tr = q.shape[0]
    qn = q * lax.rsqrt(_seg_mean(q * q, e_ref) + NORM_EPS) * qg_ref[...] * np.float32(HEAD_DIM ** -0.5 * LOG2E)
    kn = k * lax.rsqrt(_seg_mean(k * k, e_ref) + NORM_EPS) * kg_ref[...]
    z = f_ref[...].astype(F32) + fb_ref[...]
    logf2 = (jnp.minimum(z, 0.0) - jnp.log1p(jnp.exp(-jnp.abs(z)))) * np.float32(LOG2E)
    tri = tri_ref[...]
    cum = sum(_dot(tri, part) for part in _split3(logf2)) + carry_ref[...]
    carry_ref[...] = cum[tr - 1:tr, :]
    hi, mid, lo = _split3(-cum)
    bias = _dot(jnp.concatenate([hi, mid, lo], axis=1), sel_ref[...])
    m0 = _half_masks((tr, 128))
    for h in range(GH):
        p, half = divmod(h, 2)
        slot = slice(128 * h, 128 * (h + 1))
        pair = slice(128 * p, 128 * (p + 1))
        own_k, own_q, other_k, other_q = kn[:, pair], qn[:, pair], bias[:, slot], ones_ref[:, slot]
        if half == 0:
            ka = jnp.where(m0, own_k, other_k)
            qa = jnp.where(m0, own_q, other_q)
        else:
            ka = jnp.where(m0, other_k, own_k)
            qa = jnp.where(m0, other_q, own_q)
        ka_ref[:, slot] = ka.astype(BF16)
        qt_ref[0, slot, :] = qa.T.astype(BF16)
    for p in range(GH // 2):
        pair = slice(128 * p, 128 * (p + 1))
        vt_ref[0, pair, :] = v[:, pair].T.astype(BF16)


def _fox_prep_call(proj, qg, kg, fb, e512, T):
    M = proj.shape[0]
    B = M // T
    tr = 256
    nb = T // tr
    tri = jnp.tril(jnp.ones((tr, tr), BF16))
    sel = np.zeros((3 * 128, GH * 128), np.float32)
    ones = np.zeros((1, GH * 128), np.float32)
    for h in range(GH):
        off = 128 * h + (64 if h % 2 == 0 else 0)
        for part in range(3):
            sel[128 * part + OFF_FF % 128 + h, off + part] = 1.0
            ones[0, off + part] = 1.0
    cspec = lambda c: pl.BlockSpec((tr, GW), lambda b, i, c=c: (b * nb + i, c))
    return pl.pallas_call(
        _fox_prep_kernel,
        grid=(B, nb),
        in_specs=[
            cspec(OFF_FQ // GW), cspec(OFF_FK // GW), cspec(OFF_FV // GW),
            pl.BlockSpec((tr, 128), lambda b, i: (b * nb + i, OFF_FF // 128)),
            pl.BlockSpec((1, GW), lambda b, i: (0, 0)),
            pl.BlockSpec((1, GW), lambda b, i: (0, 0)),
            pl.BlockSpec((1, 128), lambda b, i: (0, 0)),
            pl.BlockSpec((GW, GW), lambda b, i: (0, 0)),
            pl.BlockSpec((tr, tr), lambda b, i: (0, 0)),
            pl.BlockSpec((3 * 128, GH * 128), lambda b, i: (0, 0)),
            pl.BlockSpec((1, GH * 128), lambda b, i: (0, 0)),
        ],
        out_specs=[
            pl.BlockSpec((1, GH * 128, tr), lambda b, i: (b, 0, i)),
            pl.BlockSpec((tr, GH * 128), lambda b, i: (b * nb + i, 0)),
            pl.BlockSpec((1, GW, tr), lambda b, i: (b, 0, i)),
        ],
        out_shape=[
            jax.ShapeDtypeStruct((B, GH * 128, T), BF16),
            jax.ShapeDtypeStruct((M, GH * 128), BF16),
            jax.ShapeDtypeStruct((B, GW, T), BF16),
        ],
        scratch_shapes=[pltpu.VMEM((1, 128), F32)],
        compiler_params=_cparams(("parallel", "arbitrary")),
        name="fox_prep",
    )(proj, proj, proj, proj, qg, kg, fb, e512, tri, jnp.asarray(sel, BF16), jnp.asarray(ones, F32))


def _fox_kernel(qt_ref, ka_ref, vt_ref, gl_ref, o_ref, m_ref, l_ref, acc_ref, *, tq):
    i = pl.program_id(1)
    m_ref[...] = jnp.full(m_ref.shape, NEG_INF, F32)
    l_ref[...] = jnp.zeros(l_ref.shape, F32)
    acc_ref[...] = jnp.zeros(acc_ref.shape, F32)

    def step(start, nk, masked):
        start = pl.multiple_of(start, tq)
        if masked:
            key = lax.broadcasted_iota(jnp.int32, (nk, tq), 0) + start
            qry = lax.broadcasted_iota(jnp.int32, (nk, tq), 1) + i * tq
            keep = key <= qry
        sts, pts, alphas = {}, {}, {}
        d1, d2 = FOX_SKEW
        for t in range(GH + d2):
            if t < GH:
                slot = slice(128 * t, 128 * (t + 1))
                sts[t] = _dot(ka_ref[pl.ds(start, nk), slot], qt_ref[0, slot, :])
            if d1 <= t < GH + d1:
                h = t - d1
                st = jnp.where(keep, sts.pop(h), NEG_INF) if masked else sts.pop(h)
                m_old = m_ref[h]
                m_new = jnp.maximum(m_old, jnp.max(st, axis=0, keepdims=True))
                alphas[h] = jnp.exp2(m_old - m_new)
                pt = jnp.exp2(st - m_new)
                l_ref[h] = alphas[h] * l_ref[h] + jnp.sum(pt, axis=0, keepdims=True)
                m_ref[h] = m_new
                pts[h] = pt.astype(BF16)
            if t >= d2:
                h = t - d2
                vth = vt_ref[0, 64 * h:64 * (h + 1), pl.ds(start, nk)]
                acc_ref[h] = alphas.pop(h) * acc_ref[h] + _dot(vth, pts.pop(h))

    def body(kb, carry):
        step(kb * (2 * tq), 2 * tq, False)
        return carry

    lax.fori_loop(0, i // 2, body, 0)

    @pl.when(i % 2 == 1)
    def _():
        step((i - 1) * tq, 2 * tq, True)

    @pl.when(i % 2 == 0)
    def _():
        step(i * tq, tq, True)

    for p in range(GH // 2):
        pair = slice(128 * p, 128 * (p + 1))
        ot = jnp.concatenate([acc_ref[2 * p] / l_ref[2 * p], acc_ref[2 * p + 1] / l_ref[2 * p + 1]], axis=0)
        o_ref[:, pair] = (ot.T * jax.nn.sigmoid(gl_ref[:, pair].astype(F32))).astype(BF16)


def _fox_call(proj, qt, ka, vt, T):
    M = proj.shape[0]
    B = M // T
    tq = 256
    nq = T // tq
    return pl.pallas_call(
        functools.partial(_fox_kernel, tq=tq),
        grid=(B, nq),
        in_specs=[
            pl.BlockSpec((1, GH * 128, tq), lambda b, i: (b, 0, i)),
            pl.BlockSpec((T, GH * 128), lambda b, i: (b, 0)),
            pl.BlockSpec((1, GW, T), lambda b, i: (b, 0, 0)),
            pl.BlockSpec((tq, GW), lambda b, i: (b * nq + i, OFF_FG // GW)),
        ],
        out_specs=pl.BlockSpec((tq, GW), lambda b, i: (b * nq + i, 0)),
        out_shape=jax.ShapeDtypeStruct((M, GW), BF16),
        scratch_shapes=[
            pltpu.VMEM((GH, 1, tq), F32),
            pltpu.VMEM((GH, 1, tq), F32),
            pltpu.VMEM((GH, HEAD_DIM, tq), F32),
        ],
        compiler_params=_cparams(("parallel", "arbitrary")),
        name="fox_attn",
    )(qt, ka, vt, proj)


def _rwkv_local_kernel(p_ref, lr_ref, pp_ref, lrp_ref, mu_ref, mulr_ref, wlr_ref, w0_ref, a0_ref,
                       kk_ref, ka_ref, rk_ref, e_ref, tri_ref,
                       m_ref, n_ref, r_ref, y0_ref, bonus_ref, g_ref, *, chunks_per_seq, nc):
    C = RWKV_CHUNK
    i = pl.program_id(0)
    has_prev = (((i * nc) % chunks_per_seq) != 0).astype(F32)
    row = lax.broadcasted_iota(jnp.int32, (nc * C, 1), 0)

    def shifted(cur_ref, prev_ref, mu):
        cur = cur_ref[...].astype(F32)
        prev_row = prev_ref[PREV_ROWS - 1:PREV_ROWS, :].astype(F32) * has_prev
        prev = jnp.where(row == 0, prev_row, pltpu.roll(cur, 1, axis=0))
        return cur + (prev - cur) * mu

    p = shifted(p_ref, pp_ref, mu_ref[...])
    lr = shifted(lr_ref, lrp_ref, mulr_ref[...])
    r = p[:, 0:GW]
    k = p[:, GW:2 * GW]
    v = p[:, 2 * GW:3 * GW]

    lane = lax.broadcasted_iota(jnp.int32, lr.shape, 1)
    z = jnp.where(lane < 32, jnp.tanh(lr), jnp.where(lane < 64, lr, jax.nn.sigmoid(lr)))
    low = _dot(z.astype(BF16), wlr_ref[...])
    lw = -np.float32(np.exp(-0.5)) * jax.nn.sigmoid(w0_ref[...] + low[:, 0:GW])
    a = jax.nn.sigmoid(a0_ref[...] + low[:, GW:2 * GW])
    g_ref[...] = low[:, 2 * GW:3 * GW]

    kk = k * kk_ref[...]
    n2 = _seg_mean(kk * kk, e_ref) * np.float32(HEAD_DIM)
    kk = kk * lax.rsqrt(jnp.maximum(n2, 1e-24))
    k = k * (1.0 + (a - 1.0) * ka_ref[...])
    b = kk * a
    bonus_ref[...] = _seg_mean(r * k * rk_ref[...], e_ref) * np.float32(HEAD_DIM) * v

    tri = tri_ref[...]
    lw_hi = lw.astype(BF16)
    lw_lo = (lw - lw_hi.astype(F32)).astype(BF16)
    L_all = _dot(tri, lw_hi) + _dot(tri, lw_lo)

    m0 = _half_masks((C, 128))
    ri = lax.broadcasted_iota(jnp.int32, (2 * C, 2 * C), 0)
    ci = lax.broadcasted_iota(jnp.int32, (2 * C, 2 * C), 1)
    strict = (ri % C) > (ci % C)
    incl = (ri % C) >= (ci % C)
    eye = ri == ci

    def stack(xp):
        return jnp.concatenate([jnp.where(m0, xp, 0.0), jnp.where(m0, 0.0, xp)], axis=0)

    chains = [(c, j) for c in range(nc) for j in range(GH // 2)]
    st = []
    for c, j in chains:
        rs = slice(c * C, (c + 1) * C)
        sl = slice(128 * j, 128 * (j + 1))
        L = L_all[rs, sl]
        Lend = L[C - 1:C, :]
        e_pos = jnp.exp(L)
        e_neg = jnp.exp(-L)
        e_hat = jnp.exp(Lend - L)
        rc, kc, bc = r[rs, sl], k[rs, sl], b[rs, sl]
        rst = stack(rc * e_pos)
        ast_b = stack(-kk[rs, sl] * jnp.exp(L - lw[rs, sl])).astype(BF16)
        st.append(dict(
            rs=rs, sl=sl, rst=rst, ast_b=ast_b, vst_b=stack(v[rs, sl]).astype(BF16),
            left=jnp.concatenate([ast_b, rst.astype(BF16)], axis=0),
            right=jnp.concatenate([stack(bc * e_neg), stack(kc * e_neg)], axis=0).astype(BF16),
            bkh=jnp.concatenate([stack(bc * e_hat), stack(kc * e_hat)], axis=0).astype(BF16),
            wc=jnp.exp(Lend)))
    for s in st:
        G = _dot_nt(s["left"], s["right"])
        a_ab = jnp.where(strict, G[0:2 * C, 0:2 * C], 0.0)
        s["a_ak"] = jnp.where(strict, G[0:2 * C, 2 * C:4 * C], 0.0).astype(BF16)
        s["a_r"] = jnp.concatenate([jnp.where(incl, G[2 * C:4 * C, 0:2 * C], 0.0),
                                    jnp.where(incl, G[2 * C:4 * C, 2 * C:4 * C], 0.0)], axis=1).astype(BF16)
        s["t_inv"] = jnp.where(eye, 1.0, a_ab)
        s["apow"] = a_ab.astype(BF16)
    for s in st:
        s["akv"] = _dot(s["a_ak"], s["vst_b"]).astype(BF16)
    for s in st:
        s["apow"] = _dot(s["apow"], s["apow"]).astype(BF16)
    for _ in range(4):
        for s in st:
            res = _dot(s["apow"], jnp.concatenate([s["apow"], s["t_inv"].astype(BF16)], axis=1))
            s["apow"] = res[:, 0:2 * C].astype(BF16)
            s["t_inv"] = s["t_inv"] + res[:, 2 * C:4 * C]
    for s in st:
        s["t_inv"] = s["t_inv"] + _dot(s["apow"], s["t_inv"].astype(BF16))
    for s in st:
        pq = _dot(s["t_inv"].astype(BF16), jnp.concatenate([s["ast_b"], s["akv"]], axis=1))
        s["rhs2"] = jnp.concatenate(
            [pq.astype(BF16), jnp.concatenate([jnp.zeros_like(s["vst_b"]), s["vst_b"]], axis=1)], axis=0)
    for s in st:
        ry = _dot(s["a_r"], s["rhs2"])
        rhat = s["rst"] + ry[:, 0:128]
        y0 = ry[:, 128:256]
        r_ref[s["rs"], s["sl"]] = (rhat[0:C, :] + rhat[C:2 * C, :]).astype(BF16)
        y0_ref[s["rs"], s["sl"]] = y0[0:C, :] + y0[C:2 * C, :]
    for (c, j), s in zip(chains, st):
        mn = _dot_tn(s["bkh"], s["rhs2"])
        m_ref[c, j] = (mn[:, 0:128] + jnp.where(eye, s["wc"], 0.0)).astype(BF16)
        n_ref[c, j] = mn[:, 128:256]


def _rwkv_local_call(proj, mu_rkv, mu_lr, wlr, w0, a0, kk, ka, rk, e512, T):
    M = proj.shape[0]
    C = RWKV_CHUNK
    nc = RWKV_CHUNKS_PER_STEP
    R = nc * C
    nchunks = M // C
    ridx = jnp.arange(R)
    tri = ((ridx[:, None] >= ridx[None, :]) & (ridx[:, None] // C == ridx[None, :] // C)).astype(BF16)
    vec = lambda n: pl.BlockSpec((1, n), lambda i: (0, 0))
    rowspec = pl.BlockSpec((R, GW), lambda i: (i, 0))
    matspec = pl.BlockSpec((nc, GH // 2, 128, 128), lambda i: (i, 0, 0, 0))
    prev = lambda i: jnp.maximum(i * (R // PREV_ROWS) - 1, 0)
    return pl.pallas_call(
        functools.partial(_rwkv_local_kernel, chunks_per_seq=T // C, nc=nc),
        grid=(nchunks // nc,),
        in_specs=[
            pl.BlockSpec((R, 3 * GW), lambda i: (i, OFF_RR // (3 * GW))),
            pl.BlockSpec((R, 256), lambda i: (i, OFF_RLR // 256)),
            pl.BlockSpec((PREV_ROWS, 3 * GW), lambda i: (prev(i), OFF_RR // (3 * GW))),
            pl.BlockSpec((PREV_ROWS, 256), lambda i: (prev(i), OFF_RLR // 256)),
            vec(3 * GW), vec(256),
            pl.BlockSpec((256, 3 * GW), lambda i: (0, 0)),
            vec(GW), vec(GW), vec(GW), vec(GW), vec(GW),
            pl.BlockSpec((GW, GW), lambda i: (0, 0)),
            pl.BlockSpec((R, R), lambda i: (0, 0)),
        ],
        out_specs=[matspec, matspec, rowspec, rowspec, rowspec, rowspec],
        out_shape=[
            jax.ShapeDtypeStruct((nchunks, GH // 2, 128, 128), BF16),
            jax.ShapeDtypeStruct((nchunks, GH // 2, 128, 128), F32),
            jax.ShapeDtypeStruct((M, GW), BF16),
            jax.ShapeDtypeStruct((M, GW), F32),
            jax.ShapeDtypeStruct((M, GW), F32),
            jax.ShapeDtypeStruct((M, GW), F32),
        ],
        compiler_params=_cparams(("parallel",)),
        name="rwkv_chunk_local",
    )(proj, proj, proj, proj, mu_rkv, mu_lr, wlr, w0, a0, kk, ka, rk, e512, tri)


def _rwkv_state_kernel(m_ref, n_ref, r_ref, y0_ref, bonus_ref, g_ref, lg_ref, lb_ref, e_ref,
                       o_ref, h_ref, *, batch, steps):
    @pl.when(pl.program_id(0) == 0)
    def _():
        h_ref[...] = jnp.zeros_like(h_ref)

    C = RWKV_CHUNK
    for c in range(steps):
        rows = slice(c * C, (c + 1) * C)
        ys = {}
        for b in range(batch):
            for j in range(GH // 2):
                sl = slice(128 * j, 128 * (j + 1))
                h = h_ref[b, j].astype(BF16)
                ys[b, j] = _dot(r_ref[b, rows, sl], h) + y0_ref[b, rows, sl]
                h_ref[b, j] = _dot(m_ref[b, c, j], h) + n_ref[b, c, j]
        for b in range(batch):
            y = jnp.concatenate([ys[b, j] for j in range(GH // 2)], axis=1)
            mean = _seg_mean(y, e_ref)
            yc = y - mean
            var = _seg_mean(yc * yc, e_ref)
            yn = yc * lax.rsqrt(var + RWKV_GN_EPS) * lg_ref[...] + lb_ref[...]
            o_ref[b, rows, :] = ((yn + bonus_ref[b, rows, :]) * g_ref[b, rows, :]).astype(BF16)


def _rwkv_state_call(mm, nn, rr, y0, bonus, g, lnx_g, lnx_b, e512, B, T):
    C = RWKV_CHUNK
    steps = RWKV_STATE_CHUNKS_PER_STEP
    nc = T // C
    mm = mm.reshape(B, nc, GH // 2, 128, 128)
    nn = nn.reshape(B, nc, GH // 2, 128, 128)
    r3 = lambda a: a.reshape(B, T, GW)
    matspec = pl.BlockSpec((B, steps, GH // 2, 128, 128), lambda c: (0, c, 0, 0, 0))
    rowspec = pl.BlockSpec((B, steps * C, GW), lambda c: (0, c, 0))
    vec = pl.BlockSpec((1, GW), lambda c: (0, 0))
    out = pl.pallas_call(
        functools.partial(_rwkv_state_kernel, batch=B, steps=steps),
        grid=(nc // steps,),
        in_specs=[matspec, matspec, rowspec, rowspec, rowspec, rowspec, vec, vec,
                  pl.BlockSpec((GW, GW), lambda c: (0, 0))],
        out_specs=rowspec,
        out_shape=jax.ShapeDtypeStruct((B, T, GW), BF16),
        scratch_shapes=[pltpu.VMEM((B, GH // 2, 128, 128), F32)],
        compiler_params=_cparams(("arbitrary",)),
        name="rwkv_state_scan",
    )(mm, nn, r3(rr), r3(y0), r3(bonus), r3(g), lnx_g, lnx_b, e512)
    return out.reshape(B * T, GW)


def _pad_cols(a, n):
    return jnp.pad(a, [(0, 0)] * (a.ndim - 1) + [(0, n - a.shape[-1])])


def _permute_in_cols(w):
    rw, sw, sg, fx = 0, 1696, 2464, 3488
    parts = [
        w[..., fx:fx + 1536],
        w[..., fx + 1544:fx + 2056],
        w[..., sg:sg + 1024],
        w[..., rw:rw + 1536],
        w[..., sw:sw + 768],
        _pad_cols(jnp.concatenate([w[..., rw + 1536:rw + 1696],
                                   w[..., fx + 1536:fx + 1544]], axis=-1), 256),
    ]
    return jnp.concatenate(parts, axis=-1)


def _tile_heads(g):
    return jnp.tile(g, GH).reshape(1, GW)


def kernel(x, c, w_mod, b_mod, norm1_g, norm2_g, w_in, w_out, rwkv_mu, rwkv_w0, rwkv_w2, rwkv_a0, rwkv_a2, rwkv_g2, rwkv_kk, rwkv_ka, rwkv_rk, rwkv_lnx_g, rwkv_lnx_b, swa_qn_g, swa_kn_g, swa_sinks, sgu_ln_g, sgu_ln_b, sgu_ws, sgu_b, fox_qn_g, fox_kn_g, fox_fb, ffn_w1, ffn_w3, ffn_w2):
    B, T, D = x.shape
    L = w_mod.shape[0]
    M = B * T

    mod = _mod_call(c, w_mod, b_mod)
    mod = mod.reshape(L, B, 6, 1, D)

    seg = jnp.arange(GW) // HEAD_DIM
    e512 = jnp.where(seg[:, None] == seg[None, :], 1.0 / HEAD_DIM, 0.0).astype(BF16)
    rows, cols = np.tril_indices(BLOCK)

    row = lambda a: a.reshape(L, 1, -1)
    tile_heads = lambda g, n: jnp.tile(g, (1, n)).reshape(L, 1, n * HEAD_DIM)
    mu_rkv = row(rwkv_mu[:, 0:3 * GW])
    mu_lr = row(_pad_cols(rwkv_mu[:, 3 * GW:], 256))
    wlr = jnp.concatenate([
        jnp.pad(rwkv_w2, ((0, 0), (0, 0), (0, 2 * GW))),
        jnp.pad(rwkv_a2, ((0, 0), (0, 0), (GW, GW))),
        jnp.pad(rwkv_g2, ((0, 0), (0, 0), (2 * GW, 0))),
        jnp.zeros((L, 256 - 160, 3 * GW), F32)], axis=1).astype(BF16)
    bias_full = jnp.repeat(jnp.swapaxes(sgu_b, 1, 2), HEAD_DIM, axis=2)
    swa_qg, swa_kg = tile_heads(swa_qn_g, GH), tile_heads(swa_kn_g, 2)
    fox_qg, fox_kg = tile_heads(fox_qn_g, GH), tile_heads(fox_kn_g, GH)
    fox_b = row(jnp.pad(fox_fb, ((0, 0), (OFF_FF % 128, 128 - GH - OFF_FF % 128))))

    x2 = x.reshape(M, D)
    for l in range(L):
        shift1, scale1, gate1, shift2, scale2, gate2 = [mod[l, :, s] for s in range(6)]
        w_in_l = _permute_in_cols(w_in[l]).astype(BF16)
        proj = _in_call(x2, norm1_g[l].reshape(1, D), scale1, shift1, w_in_l, T)

        vec = lambda a: a.reshape(1, GW)
        mm, nn, rr, y0, bonus, g = _rwkv_local_call(
            proj, mu_rkv[l], mu_lr[l], wlr[l], vec(rwkv_w0[l]), vec(rwkv_a0[l]), vec(rwkv_kk[l]),
            vec(rwkv_ka[l]), vec(rwkv_rk[l]), e512, T)
        ya = _rwkv_state_call(mm, nn, rr, y0, bonus, g, vec(rwkv_lnx_g[l]), vec(rwkv_lnx_b[l]), e512, B, T)

        yb = _swa_call(proj, swa_sinks[l], swa_qg[l], swa_kg[l], e512, T)

        w_s = jnp.zeros((GH, BLOCK, BLOCK), F32).at[:, rows, cols].set(sgu_ws[l])
        ws_cat = jnp.concatenate([w_s[0::2], w_s[1::2]], axis=2).astype(BF16)
        yc = _sgu_call(proj, vec(sgu_ln_g[l]), vec(sgu_ln_b[l]), ws_cat, bias_full[l])

        qt, ka, vt = _fox_prep_call(proj, fox_qg[l], fox_kg[l], fox_b[l], e512, T)
        yd = _fox_call(proj, qt, ka, vt, T)

        x2 = _out_call(ya, yb, yc, yd, w_out[l].astype(BF16), x2, gate1, T)
        x2 = _ffn_call(x2, norm2_g[l].reshape(1, D), scale2, shift2, gate2,
                       ffn_w1[l].astype(BF16), ffn_w3[l].astype(BF16), ffn_w2[l].astype(BF16), T)
    return x2.reshape(B, T, D)
```

```python
import functools

import numpy as np
import jax
import jax.numpy as jnp
from jax import lax
from jax.experimental import pallas as pl
from jax.experimental.pallas import tpu as pltpu

F32 = jnp.float32
BF16 = jnp.bfloat16

D_MODEL = 2048
HEAD_DIM = 64
GW = 512
GH = 8
BLOCK = 128
NORM_EPS = 1e-6
LN_EPS = 1e-5
NEG_INF = -1e30
RWKV_GN_EPS = 64e-5
NORM_ROWS = 16
PREV_ROWS = 16
SWA_GROUPS = 4
FOX_SKEW = (3, 6)
RWKV_CHUNK = 64
RWKV_CHUNKS_PER_STEP = 4
RWKV_STATE_CHUNKS_PER_STEP = 4
LOG2E = float(np.log2(np.e))
FFN_HIDDEN = 5632

OFF_FQ, OFF_FK, OFF_FV, OFF_FG = 0, 512, 1024, 1536
OFF_SU, OFF_SV = 2048, 2560
OFF_RR = 3072
OFF_WQ = 4608
OFF_WKV = 5120
OFF_RLR = 5376
OFF_FF = OFF_RLR + 160
NP = 5632

VMEM_LIMIT = 56 * 1024 * 1024
IN_PROJ_VMEM_LIMIT = 60 * 1024 * 1024


def _cparams(sem, vmem_limit=VMEM_LIMIT):
    return pltpu.CompilerParams(dimension_semantics=sem, vmem_limit_bytes=vmem_limit)


def _dot(a, b):
    return jnp.dot(a, b, preferred_element_type=F32)


def _dot_nt(a, b):
    return lax.dot_general(a, b, (((1,), (1,)), ((), ())), preferred_element_type=F32)


def _dot_tn(a, b):
    return lax.dot_general(a, b, (((0,), (0,)), ((), ())), preferred_element_type=F32)


def _seg_mean(x, e_ref):
    return _dot(x.astype(BF16), e_ref[...])


def _split3(x):
    hi = x.astype(BF16)
    r1 = x - hi.astype(F32)
    mid = r1.astype(BF16)
    lo = (r1 - mid.astype(F32)).astype(BF16)
    return hi, mid, lo


def _half_masks(shape):
    lane = lax.broadcasted_iota(jnp.int32, shape, len(shape) - 1)
    m0 = (lane % 128) < 64
    return m0


def _mod_kernel(c_ref, w_ref, b_ref, o_ref):
    c = c_ref[...]
    ca = (c * jax.nn.sigmoid(c)).astype(BF16)
    o_ref[0] = _dot(ca, w_ref[0].astype(BF16)) + b_ref[0]


def _mod_call(c, w_mod, b_mod):
    L, D, N = w_mod.shape
    B = c.shape[0]
    tn = 1024
    return pl.pallas_call(
        _mod_kernel,
        grid=(L, N // tn),
        in_specs=[
            pl.BlockSpec((B, D), lambda l, j: (0, 0)),
            pl.BlockSpec((1, D, tn), lambda l, j: (l, 0, j)),
            pl.BlockSpec((1, 1, tn), lambda l, j: (l, 0, j)),
        ],
        out_specs=pl.BlockSpec((1, B, tn), lambda l, j: (l, 0, j)),
        out_shape=jax.ShapeDtypeStruct((L, B, N), F32),
        compiler_params=_cparams(("parallel", "parallel")),
        name="adaln_mod",
    )(c, w_mod, b_mod.reshape(L, 1, N))


def _norm_modulate(x_ref, g_ref, sc_ref, sh_ref, h_ref):
    gs = g_ref[...] * (1.0 + sc_ref[0])
    sh = sh_ref[0]

    def body(c, carry):
        rows = pl.ds(pl.multiple_of(c * NORM_ROWS, NORM_ROWS), NORM_ROWS)
        x = x_ref[rows, :]
        ms = jnp.mean(x * x, axis=-1, keepdims=True)
        h_ref[rows, :] = (x * lax.rsqrt(ms + NORM_EPS) * gs + sh).astype(BF16)
        return carry

    lax.fori_loop(0, x_ref.shape[0] // NORM_ROWS, body, 0, unroll=8)


def _in_kernel(x_ref, g_ref, sc_ref, sh_ref, w_ref, o_ref, h_ref):
    @pl.when(pl.program_id(1) == 0)
    def _():
        _norm_modulate(x_ref, g_ref, sc_ref, sh_ref, h_ref)

    o_ref[...] = _dot(h_ref[...], w_ref[...]).astype(o_ref.dtype)


def _in_call(x2, g, scale, shift, w, T):
    M, D = x2.shape
    n = w.shape[1]
    tm, tn = 1024, 2816
    return pl.pallas_call(
        _in_kernel,
        grid=(M // tm, n // tn),
        in_specs=[
            pl.BlockSpec((tm, D), lambda i, j: (i, 0)),
            pl.BlockSpec((1, D), lambda i, j: (0, 0)),
            pl.BlockSpec((1, 1, D), lambda i, j: (i * tm // T, 0, 0)),
            pl.BlockSpec((1, 1, D), lambda i, j: (i * tm // T, 0, 0)),
            pl.BlockSpec((D, tn), lambda i, j: (0, j)),
        ],
        out_specs=pl.BlockSpec((tm, tn), lambda i, j: (i, j)),
        out_shape=jax.ShapeDtypeStruct((M, n), BF16),
        scratch_shapes=[pltpu.VMEM((tm, D), BF16)],
        compiler_params=_cparams(("parallel", "arbitrary"), IN_PROJ_VMEM_LIMIT),
        name="norm_in_proj",
    )(x2, g, scale, shift, w)


def _out_kernel(ya_ref, yb_ref, yc_ref, yd_ref, w_ref, x_ref, gt_ref, o_ref):
    acc = _dot(ya_ref[...], w_ref[0:GW, :])
    acc += _dot(yb_ref[...], w_ref[GW:2 * GW, :])
    acc += _dot(yc_ref[...], w_ref[2 * GW:3 * GW, :])
    acc += _dot(yd_ref[...], w_ref[3 * GW:4 * GW, :])
    o_ref[...] = x_ref[...] + gt_ref[0] * acc


def _out_call(ya, yb, yc, yd, w, x2, gate, T):
    M, D = x2.shape
    tm = 512
    yspec = pl.BlockSpec((tm, GW), lambda i: (i, 0))
    return pl.pallas_call(
        _out_kernel,
        grid=(M // tm,),
        in_specs=[
            yspec, yspec, yspec, yspec,
            pl.BlockSpec((D, D), lambda i: (0, 0)),
            pl.BlockSpec((tm, D), lambda i: (i, 0)),
            pl.BlockSpec((1, 1, D), lambda i: (i * tm // T, 0, 0)),
        ],
        out_specs=pl.BlockSpec((tm, D), lambda i: (i, 0)),
        out_shape=jax.ShapeDtypeStruct((M, D), F32),
        compiler_params=_cparams(("parallel",)),
        name="out_proj_residual",
    )(ya, yb, yc, yd, w, x2, gate)


def _ffn_kernel(x_ref, g_ref, sc_ref, sh_ref, gt_ref, w1_ref, w3_ref, w2_ref, o_ref, h_ref):
    @pl.when(pl.program_id(1) == 0)
    def _():
        _norm_modulate(x_ref, g_ref, sc_ref, sh_ref, h_ref)
        o_ref[...] = x_ref[...]

    h = h_ref[...]
    half = w1_ref.shape[1] // 2
    fs = []
    for c in range(2):
        cols = slice(c * half, (c + 1) * half)
        a = _dot(h, w1_ref[:, cols])
        b = _dot(h, w3_ref[:, cols])
        fs.append((a * jax.nn.sigmoid(a) * b).astype(BF16))
    o_ref[...] += gt_ref[0] * _dot(jnp.concatenate(fs, axis=1), w2_ref[...])


def _ffn_call(x2, g, scale, shift, gate, w1, w3, w2, T):
    M, D = x2.shape
    F = w1.shape[1]
    tm, tf = 1024, 512
    bspec = pl.BlockSpec((1, 1, D), lambda i, j: (i * tm // T, 0, 0))
    return pl.pallas_call(
        _ffn_kernel,
        grid=(M // tm, F // tf),
        in_specs=[
            pl.BlockSpec((tm, D), lambda i, j: (i, 0)),
            pl.BlockSpec((1, D), lambda i, j: (0, 0)),
            bspec, bspec, bspec,
            pl.BlockSpec((D, tf), lambda i, j: (0, j)),
            pl.BlockSpec((D, tf), lambda i, j: (0, j)),
            pl.BlockSpec((tf, D), lambda i, j: (j, 0)),
        ],
        out_specs=pl.BlockSpec((tm, D), lambda i, j: (i, 0)),
        out_shape=jax.ShapeDtypeStruct((M, D), F32),
        scratch_shapes=[pltpu.VMEM((tm, D), BF16)],
        compiler_params=_cparams(("parallel", "arbitrary")),
        name="swiglu_ffn",
    )(x2, g, scale, shift, gate, w1, w3, w2)


def _gelu(x):
    return 0.5 * x * (1.0 + lax.erf(x * np.float32(1.0 / np.sqrt(2.0))))


def _sgu_kernel(u_ref, v_ref, g_ref, b_ref, ws_ref, bias_ref, o_ref, *, chunks):
    m0 = _half_masks((BLOCK, 128))
    for c in range(chunks):
        rows = slice(c * BLOCK, (c + 1) * BLOCK)
        u = _gelu(u_ref[rows, :].astype(F32))
        v = _gelu(v_ref[rows, :].astype(F32))
        mu = jnp.mean(v, axis=-1, keepdims=True)
        vc = v - mu
        var = jnp.mean(vc * vc, axis=-1, keepdims=True)
        vn = vc * lax.rsqrt(var + LN_EPS) * g_ref[...] + b_ref[...]
        outs = []
        for j in range(GH // 2):
            vp = vn[:, 128 * j:128 * (j + 1)]
            stacked = jnp.concatenate(
                [jnp.where(m0, vp, 0.0), jnp.where(m0, 0.0, vp)], axis=0).astype(BF16)
            outs.append(_dot(ws_ref[j], stacked))
        z = jnp.concatenate(outs, axis=1) + bias_ref[...]
        o_ref[rows, :] = (u * z).astype(BF16)


def _sgu_call(proj, ln_g, ln_b, ws_cat, bias_full):
    M = proj.shape[0]
    chunks = 4
    tr = chunks * BLOCK
    return pl.pallas_call(
        functools.partial(_sgu_kernel, chunks=chunks),
        grid=(M // tr,),
        in_specs=[
            pl.BlockSpec((tr, GW), lambda i: (i, OFF_SU // GW)),
            pl.BlockSpec((tr, GW), lambda i: (i, OFF_SV // GW)),
            pl.BlockSpec((1, GW), lambda i: (0, 0)),
            pl.BlockSpec((1, GW), lambda i: (0, 0)),
            pl.BlockSpec((GH // 2, BLOCK, 2 * BLOCK), lambda i: (0, 0, 0)),
            pl.BlockSpec((BLOCK, GW), lambda i: (0, 0)),
        ],
        out_specs=pl.BlockSpec((tr, GW), lambda i: (i, 0)),
        out_shape=jax.ShapeDtypeStruct((M, GW), BF16),
        compiler_params=_cparams(("parallel",)),
        name="sgu_mix",
    )(proj, proj, ln_g, ln_b, ws_cat, bias_full)


def _swa_kernel(sink_ref, q_ref, kv_ref, kvp_ref, qg_ref, kg_ref, e_ref, o_ref, *, blocks, blocks_per_seq):
    i = pl.program_id(0)
    m0q = _half_masks((BLOCK, 128))
    m0k = _half_masks((2 * BLOCK, 128))
    row = lax.broadcasted_iota(jnp.int32, (BLOCK, 2 * BLOCK), 0)
    col = lax.broadcasted_iota(jnp.int32, (BLOCK, 2 * BLOCK), 1)
    band = (col <= row + BLOCK) & (col > row)
    e128 = e_ref.at[0:128, 0:128]

    kv_all = jnp.concatenate([kvp_ref[...], kv_ref[...]], axis=0).astype(F32)
    for r in range(blocks):
        first = ((i * blocks + r) % blocks_per_seq) == 0
        mask = band & (col >= jnp.where(first, BLOCK, 0))
        q = q_ref[r * BLOCK:(r + 1) * BLOCK, :].astype(F32)
        qn = q * lax.rsqrt(_seg_mean(q * q, e_ref) + NORM_EPS) * qg_ref[...] * np.float32(HEAD_DIM ** -0.5 * LOG2E)
        kv = kv_all[r * BLOCK:(r + 2) * BLOCK, :]
        k = kv[:, 0:128]
        v = kv[:, 128:256]
        kn = k * lax.rsqrt(_seg_mean(k * k, e128) + NORM_EPS) * kg_ref[...]
        kn_sw = pltpu.roll(kn, 64, axis=1)
        v_sw = pltpu.roll(v, 64, axis=1)
        kdup = [jnp.where(m0k, kn, kn_sw).astype(BF16), jnp.where(m0k, kn_sw, kn).astype(BF16)]
        vdup = [jnp.where(m0k, v, v_sw), jnp.where(m0k, v_sw, v)]
        vmsk = [[jnp.where(m0k, vd, 0.0).astype(BF16), jnp.where(m0k, 0.0, vd).astype(BF16)] for vd in vdup]
        scores = []
        for h in range(GH):
            qp = qn[:, 128 * (h // 2):128 * (h // 2 + 1)]
            qm = (jnp.where(m0q, qp, 0.0) if h % 2 == 0 else jnp.where(m0q, 0.0, qp)).astype(BF16)
            scores.append(_dot_nt(qm, kdup[h // SWA_GROUPS]))
        probs, inv = [], []
        for h in range(GH):
            s = jnp.where(mask, scores[h], NEG_INF)
            sink = sink_ref[h] * np.float32(LOG2E)
            m = jnp.maximum(jnp.max(s, axis=-1, keepdims=True), sink)
            p = jnp.exp2(s - m)
            inv.append(1.0 / (jnp.sum(p, axis=-1, keepdims=True) + jnp.exp2(sink - m)))
            probs.append(p.astype(BF16))
        outs = []
        for j in range(GH // 2):
            g = (2 * j) // SWA_GROUPS
            outs.append(_dot(probs[2 * j], vmsk[g][0]) * inv[2 * j]
                        + _dot(probs[2 * j + 1], vmsk[g][1]) * inv[2 * j + 1])
        o_ref[r * BLOCK:(r + 1) * BLOCK, :] = jnp.concatenate(outs, axis=1).astype(BF16)


def _swa_call(proj, sinks, qg, kg, e512, T):
    M = proj.shape[0]
    blocks = 4
    tr = blocks * BLOCK
    return pl.pallas_call(
        functools.partial(_swa_kernel, blocks=blocks, blocks_per_seq=T // BLOCK),
        grid=(M // tr,),
        in_specs=[
            pl.BlockSpec(memory_space=pltpu.SMEM),
            pl.BlockSpec((tr, GW), lambda i: (i, OFF_WQ // GW)),
            pl.BlockSpec((tr, 256), lambda i: (i, OFF_WKV // 256)),
            pl.BlockSpec((BLOCK, 256), lambda i: (jnp.maximum(i * blocks - 1, 0), OFF_WKV // 256)),
            pl.BlockSpec((1, GW), lambda i: (0, 0)),
            pl.BlockSpec((1, 128), lambda i: (0, 0)),
            pl.BlockSpec((GW, GW), lambda i: (0, 0)),
        ],
        out_specs=pl.BlockSpec((tr, GW), lambda i: (i, 0)),
        out_shape=jax.ShapeDtypeStruct((M, GW), BF16),
        compiler_params=_cparams(("parallel",)),
        name="swa_mix",
    )(sinks, proj, proj, proj, qg, kg, e512)


def _fox_prep_kernel(q_ref, k_ref, v_ref, f_ref, qg_ref, kg_ref, fb_ref, e_ref, tri_ref, sel_ref, ones_ref,
                     qt_ref, ka_ref, vt_ref, carry_ref):
    @pl.when(pl.program_id(1) == 0)
    def _():
        carry_ref[...] = jnp.zeros_like(carry_ref)

    q = q_ref[...].astype(F32)
    k = k_ref[...].astype(F32)
    v = v_ref[...].astype(F32)
    tr = q.shape[0]
    qn = q * lax.rsqrt(_seg_mean(q * q, e_ref) + NORM_EPS) * qg_ref[...] * np.float32(HEAD_DIM ** -0.5 * LOG2E)
    kn = k * lax.rsqrt(_seg_mean(k * k, e_ref) + NORM_EPS) * kg_ref[...]
    z = f_ref[...].astype(F32) + fb_ref[...]
    logf2 = (jnp.minimum(z, 0.0) - jnp.log1p(jnp.exp(-jnp.abs(z)))) * np.float32(LOG2E)
    tri = tri_ref[...]
    cum = sum(_dot(tri, part) for part in _split3(logf2)) + carry_ref[...]
    carry_ref[...] = cum[tr - 1:tr, :]
    hi, mid, lo = _split3(-cum)
    bias = _dot(jnp.concatenate([hi, mid, lo], axis=1), sel_ref[...])
    m0 = _half_masks((tr, 128))
    for h in range(GH):
        p, half = divmod(h, 2)
        slot = slice(128 * h, 128 * (h + 1))
        pair = slice(128 * p, 128 * (p + 1))
        own_k, own_q, other_k, other_q = kn[:, pair], qn[:, pair], bias[:, slot], ones_ref[:, slot]
        if half == 0:
            ka = jnp.where(m0, own_k, other_k)
            qa = jnp.where(m0, own_q, other_q)
        else:
            ka = jnp.where(m0, other_k, own_k)
            qa = jnp.where(m0, other_q, own_q)
        ka_ref[:, slot] = ka.astype(BF16)
        qt_ref[0, slot, :] = qa.T.astype(BF16)
    for p in range(GH // 2):
        pair = slice(128 * p, 128 * (p + 1))
        vt_ref[0, pair, :] = v[:, pair].T.astype(BF16)


def _fox_prep_call(proj, qg, kg, fb, e512, T):
    M = proj.shape[0]
    B = M // T
    tr = 256
    nb = T // tr
    tri = jnp.tril(jnp.ones((tr, tr), BF16))
    sel = np.zeros((3 * 128, GH * 128), np.float32)
    ones = np.zeros((1, GH * 128), np.float32)
    for h in range(GH):
        off = 128 * h + (64 if h % 2 == 0 else 0)
        for part in range(3):
            sel[128 * part + OFF_FF % 128 + h, off + part] = 1.0
            ones[0, off + part] = 1.0
    cspec = lambda c: pl.BlockSpec((tr, GW), lambda b, i, c=c: (b * nb + i, c))
    return pl.pallas_call(
        _fox_prep_kernel,
        grid=(B, nb),
        in_specs=[
            cspec(OFF_FQ // GW), cspec(OFF_FK // GW), cspec(OFF_FV // GW),
            pl.BlockSpec((tr, 128), lambda b, i: (b * nb + i, OFF_FF // 128)),
            pl.BlockSpec((1, GW), lambda b, i: (0, 0)),
            pl.BlockSpec((1, GW), lambda b, i: (0, 0)),
            pl.BlockSpec((1, 128), lambda b, i: (0, 0)),
            pl.BlockSpec((GW, GW), lambda b, i: (0, 0)),
            pl.BlockSpec((tr, tr), lambda b, i: (0, 0)),
            pl.BlockSpec((3 * 128, GH * 128), lambda b, i: (0, 0)),
            pl.BlockSpec((1, GH * 128), lambda b, i: (0, 0)),
        ],
        out_specs=[
            pl.BlockSpec((1, GH * 128, tr), lambda b, i: (b, 0, i)),
            pl.BlockSpec((tr, GH * 128), lambda b, i: (b * nb + i, 0)),
            pl.BlockSpec((1, GW, tr), lambda b, i: (b, 0, i)),
        ],
        out_shape=[
            jax.ShapeDtypeStruct((B, GH * 128, T), BF16),
            jax.ShapeDtypeStruct((M, GH * 128), BF16),
            jax.ShapeDtypeStruct((B, GW, T), BF16),
        ],
        scratch_shapes=[pltpu.VMEM((1, 128), F32)],
        compiler_params=_cparams(("parallel", "arbitrary")),
        name="fox_prep",
    )(proj, proj, proj, proj, qg, kg, fb, e512, tri, jnp.asarray(sel, BF16), jnp.asarray(ones, F32))


def _fox_kernel(qt_ref, ka_ref, vt_ref, gl_ref, o_ref, m_ref, l_ref, acc_ref, *, tq):
    i = pl.program_id(1)
    m_ref[...] = jnp.full(m_ref.shape, NEG_INF, F32)
    l_ref[...] = jnp.zeros(l_ref.shape, F32)
    acc_ref[...] = jnp.zeros(acc_ref.shape, F32)

    def step(start, nk, masked):
        start = pl.multiple_of(start, tq)
        if masked:
            key = lax.broadcasted_iota(jnp.int32, (nk, tq), 0) + start
            qry = lax.broadcasted_iota(jnp.int32, (nk, tq), 1) + i * tq
            keep = key <= qry
        sts, pts, alphas = {}, {}, {}
        d1, d2 = FOX_SKEW
        for t in range(GH + d2):
            if t < GH:
                slot = slice(128 * t, 128 * (t + 1))
                sts[t] = _dot(ka_ref[pl.ds(start, nk), slot], qt_ref[0, slot, :])
            if d1 <= t < GH + d1:
                h = t - d1
                st = jnp.where(keep, sts.pop(h), NEG_INF) if masked else sts.pop(h)
                m_old = m_ref[h]
                m_new = jnp.maximum(m_old, jnp.max(st, axis=0, keepdims=True))
                alphas[h] = jnp.exp2(m_old - m_new)
                pt = jnp.exp2(st - m_new)
                l_ref[h] = alphas[h] * l_ref[h] + jnp.sum(pt, axis=0, keepdims=True)
                m_ref[h] = m_new
                pts[h] = pt.astype(BF16)
            if t >= d2:
                h = t - d2
                vth = vt_ref[0, 64 * h:64 * (h + 1), pl.ds(start, nk)]
                acc_ref[h] = alphas.pop(h) * acc_ref[h] + _dot(vth, pts.pop(h))

    def body(kb, carry):
        step(kb * (2 * tq), 2 * tq, False)
        return carry

    lax.fori_loop(0, i // 2, body, 0)

    @pl.when(i % 2 == 1)
    def _():
        step((i - 1) * tq, 2 * tq, True)

    @pl.when(i % 2 == 0)
    def _():
        step(i * tq, tq, True)

    for p in range(GH // 2):
        pair = slice(128 * p, 128 * (p + 1))
        ot = jnp.concatenate([acc_ref[2 * p] / l_ref[2 * p], acc_ref[2 * p + 1] / l_ref[2 * p + 1]], axis=0)
        o_ref[:, pair] = (ot.T * jax.nn.sigmoid(gl_ref[:, pair].astype(F32))).astype(BF16)


def _fox_call(proj, qt, ka, vt, T):
    M = proj.shape[0]
    B = M // T
    tq = 256
    nq = T // tq
    return pl.pallas_call(
        functools.partial(_fox_kernel, tq=tq),
        grid=(B, nq),
        in_specs=[
            pl.BlockSpec((1, GH * 128, tq), lambda b, i: (b, 0, i)),
            pl.BlockSpec((T, GH * 128), lambda b, i: (b, 0)),
            pl.BlockSpec((1, GW, T), lambda b, i: (b, 0, 0)),
            pl.BlockSpec((tq, GW), lambda b, i: (b * nq + i, OFF_FG // GW)),
        ],
        out_specs=pl.BlockSpec((tq, GW), lambda b, i: (b * nq + i, 0)),
        out_shape=jax.ShapeDtypeStruct((M, GW), BF16),
        scratch_shapes=[
            pltpu.VMEM((GH, 1, tq), F32),
            pltpu.VMEM((GH, 1, tq), F32),
            pltpu.VMEM((GH, HEAD_DIM, tq), F32),
        ],
        compiler_params=_cparams(("parallel", "arbitrary")),
        name="fox_attn",
    )(qt, ka, vt, proj)


def _rwkv_local_kernel(p_ref, lr_ref, pp_ref, lrp_ref, mu_ref, mulr_ref, wlr_ref, w0_ref, a0_ref,
                       kk_ref, ka_ref, rk_ref, e_ref, tri_ref,
                       m_ref, n_ref, r_ref, y0_ref, bonus_ref, g_ref, *, chunks_per_seq, nc):
    C = RWKV_CHUNK
    i = pl.program_id(0)
    has_prev = (((i * nc) % chunks_per_seq) != 0).astype(F32)
    row = lax.broadcasted_iota(jnp.int32, (nc * C, 1), 0)

    def shifted(cur_ref, prev_ref, mu):
        cur = cur_ref[...].astype(F32)
        prev_row = prev_ref[PREV_ROWS - 1:PREV_ROWS, :].astype(F32) * has_prev
        prev = jnp.where(row == 0, prev_row, pltpu.roll(cur, 1, axis=0))
        return cur + (prev - cur) * mu

    p = shifted(p_ref, pp_ref, mu_ref[...])
    lr = shifted(lr_ref, lrp_ref, mulr_ref[...])
    r = p[:, 0:GW]
    k = p[:, GW:2 * GW]
    v = p[:, 2 * GW:3 * GW]

    lane = lax.broadcasted_iota(jnp.int32, lr.shape, 1)
    z = jnp.where(lane < 32, jnp.tanh(lr), jnp.where(lane < 64, lr, jax.nn.sigmoid(lr)))
    low = _dot(z.astype(BF16), wlr_ref[...])
    lw = -np.float32(np.exp(-0.5)) * jax.nn.sigmoid(w0_ref[...] + low[:, 0:GW])
    a = jax.nn.sigmoid(a0_ref[...] + low[:, GW:2 * GW])
    g_ref[...] = low[:, 2 * GW:3 * GW]

    kk = k * kk_ref[...]
    n2 = _seg_mean(kk * kk, e_ref) * np.float32(HEAD_DIM)
    kk = kk * lax.rsqrt(jnp.maximum(n2, 1e-24))
    k = k * (1.0 + (a - 1.0) * ka_ref[...])
    b = kk * a
    bonus_ref[...] = _seg_mean(r * k * rk_ref[...], e_ref) * np.float32(HEAD_DIM) * v

    tri = tri_ref[...]
    lw_hi = lw.astype(BF16)
    lw_lo = (lw - lw_hi.astype(F32)).astype(BF16)
    L_all = _dot(tri, lw_hi) + _dot(tri, lw_lo)

    m0 = _half_masks((C, 128))
    ri = lax.broadcasted_iota(jnp.int32, (2 * C, 2 * C), 0)
    ci = lax.broadcasted_iota(jnp.int32, (2 * C, 2 * C), 1)
    strict = (ri % C) > (ci % C)
    incl = (ri % C) >= (ci % C)
    eye = ri == ci

    def stack(xp):
        return jnp.concatenate([jnp.where(m0, xp, 0.0), jnp.where(m0, 0.0, xp)], axis=0)

    chains = [(c, j) for c in range(nc) for j in range(GH // 2)]
    st = []
    for c, j in chains:
        rs = slice(c * C, (c + 1) * C)
        sl = slice(128 * j, 128 * (j + 1))
        L = L_all[rs, sl]
        Lend = L[C - 1:C, :]
        e_pos = jnp.exp(L)
        e_neg = jnp.exp(-L)
        e_hat = jnp.exp(Lend - L)
        rc, kc, bc = r[rs, sl], k[rs, sl], b[rs, sl]
        rst = stack(rc * e_pos)
        ast_b = stack(-kk[rs, sl] * jnp.exp(L - lw[rs, sl])).astype(BF16)
        st.append(dict(
            rs=rs, sl=sl, rst=rst, ast_b=ast_b, vst_b=stack(v[rs, sl]).astype(BF16),
            left=jnp.concatenate([ast_b, rst.astype(BF16)], axis=0),
            right=jnp.concatenate([stack(bc * e_neg), stack(kc * e_neg)], axis=0).astype(BF16),
            bkh=jnp.concatenate([stack(bc * e_hat), stack(kc * e_hat)], axis=0).astype(BF16),
            wc=jnp.exp(Lend)))
    for s in st:
        G = _dot_nt(s["left"], s["right"])
        a_ab = jnp.where(strict, G[0:2 * C, 0:2 * C], 0.0)
        s["a_ak"] = jnp.where(strict, G[0:2 * C, 2 * C:4 * C], 0.0).astype(BF16)
        s["a_r"] = jnp.concatenate([jnp.where(incl, G[2 * C:4 * C, 0:2 * C], 0.0),
                                    jnp.where(incl, G[2 * C:4 * C, 2 * C:4 * C], 0.0)], axis=1).astype(BF16)
        s["t_inv"] = jnp.where(eye, 1.0, a_ab)
        s["apow"] = a_ab.astype(BF16)
    for s in st:
        s["akv"] = _dot(s["a_ak"], s["vst_b"]).astype(BF16)
    for s in st:
        s["apow"] = _dot(s["apow"], s["apow"]).astype(BF16)
    for _ in range(4):
        for s in st:
            res = _dot(s["apow"], jnp.concatenate([s["apow"], s["t_inv"].astype(BF16)], axis=1))
            s["apow"] = res[:, 0:2 * C].astype(BF16)
            s["t_inv"] = s["t_inv"] + res[:, 2 * C:4 * C]
    for s in st:
        s["t_inv"] = s["t_inv"] + _dot(s["apow"], s["t_inv"].astype(BF16))
    for s in st:
        pq = _dot(s["t_inv"].astype(BF16), jnp.concatenate([s["ast_b"], s["akv"]], axis=1))
        s["rhs2"] = jnp.concatenate(
            [pq.astype(BF16), jnp.concatenate([jnp.zeros_like(s["vst_b"]), s["vst_b"]], axis=1)], axis=0)
    for s in st:
        ry = _dot(s["a_r"], s["rhs2"])
        rhat = s["rst"] + ry[:, 0:128]
        y0 = ry[:, 128:256]
        r_ref[s["rs"], s["sl"]] = (rhat[0:C, :] + rhat[C:2 * C, :]).astype(BF16)
        y0_ref[s["rs"], s["sl"]] = y0[0:C, :] + y0[C:2 * C, :]
    for (c, j), s in zip(chains, st):
        mn = _dot_tn(s["bkh"], s["rhs2"])
        m_ref[c, j] = (mn[:, 0:128] + jnp.where(eye, s["wc"], 0.0)).astype(BF16)
        n_ref[c, j] = mn[:, 128:256]


def _rwkv_local_call(proj, mu_rkv, mu_lr, wlr, w0, a0, kk, ka, rk, e512, T):
    M = proj.shape[0]
    C = RWKV_CHUNK
    nc = RWKV_CHUNKS_PER_STEP
    R = nc * C
    nchunks = M // C
    ridx = jnp.arange(R)
    tri = ((ridx[:, None] >= ridx[None, :]) & (ridx[:, None] // C == ridx[None, :] // C)).astype(BF16)
    vec = lambda n: pl.BlockSpec((1, n), lambda i: (0, 0))
    rowspec = pl.BlockSpec((R, GW), lambda i: (i, 0))
    matspec = pl.BlockSpec((nc, GH // 2, 128, 128), lambda i: (i, 0, 0, 0))
    prev = lambda i: jnp.maximum(i * (R // PREV_ROWS) - 1, 0)
    return pl.pallas_call(
        functools.partial(_rwkv_local_kernel, chunks_per_seq=T // C, nc=nc),
        grid=(nchunks // nc,),
        in_specs=[
            pl.BlockSpec((R, 3 * GW), lambda i: (i, OFF_RR // (3 * GW))),
            pl.BlockSpec((R, 256), lambda i: (i, OFF_RLR // 256)),
            pl.BlockSpec((PREV_ROWS, 3 * GW), lambda i: (prev(i), OFF_RR // (3 * GW))),
            pl.BlockSpec((PREV_ROWS, 256), lambda i: (prev(i), OFF_RLR // 256)),
            vec(3 * GW), vec(256),
            pl.BlockSpec((256, 3 * GW), lambda i: (0, 0)),
            vec(GW), vec(GW), vec(GW), vec(GW), vec(GW),
            pl.BlockSpec((GW, GW), lambda i: (0, 0)),
            pl.BlockSpec((R, R), lambda i: (0, 0)),
        ],
        out_specs=[matspec, matspec, rowspec, rowspec, rowspec, rowspec],
        out_shape=[
            jax.ShapeDtypeStruct((nchunks, GH // 2, 128, 128), BF16),
            jax.ShapeDtypeStruct((nchunks, GH // 2, 128, 128), F32),
            jax.ShapeDtypeStruct((M, GW), BF16),
            jax.ShapeDtypeStruct((M, GW), F32),
            jax.ShapeDtypeStruct((M, GW), F32),
            jax.ShapeDtypeStruct((M, GW), F32),
        ],
        compiler_params=_cparams(("parallel",)),
        name="rwkv_chunk_local",
    )(proj, proj, proj, proj, mu_rkv, mu_lr, wlr, w0, a0, kk, ka, rk, e512, tri)


def _rwkv_state_kernel(m_ref, n_ref, r_ref, y0_ref, bonus_ref, g_ref, lg_ref, lb_ref, e_ref,
                       o_ref, h_ref, *, batch, steps):
    @pl.when(pl.program_id(0) == 0)
    def _():
        h_ref[...] = jnp.zeros_like(h_ref)

    C = RWKV_CHUNK
    for c in range(steps):
        rows = slice(c * C, (c + 1) * C)
        ys = {}
        for b in range(batch):
            for j in range(GH // 2):
                sl = slice(128 * j, 128 * (j + 1))
                h = h_ref[b, j].astype(BF16)
                ys[b, j] = _dot(r_ref[b, rows, sl], h) + y0_ref[b, rows, sl]
                h_ref[b, j] = _dot(m_ref[b, c, j], h) + n_ref[b, c, j]
        for b in range(batch):
            y = jnp.concatenate([ys[b, j] for j in range(GH // 2)], axis=1)
            mean = _seg_mean(y, e_ref)
            yc = y - mean
            var = _seg_mean(yc * yc, e_ref)
            yn = yc * lax.rsqrt(var + RWKV_GN_EPS) * lg_ref[...] + lb_ref[...]
            o_ref[b, rows, :] = ((yn + bonus_ref[b, rows, :]) * g_ref[b, rows, :]).astype(BF16)


def _rwkv_state_call(mm, nn, rr, y0, bonus, g, lnx_g, lnx_b, e512, B, T):
    C = RWKV_CHUNK
    steps = RWKV_STATE_CHUNKS_PER_STEP
    nc = T // C
    mm = mm.reshape(B, nc, GH // 2, 128, 128)
    nn = nn.reshape(B, nc, GH // 2, 128, 128)
    r3 = lambda a: a.reshape(B, T, GW)
    matspec = pl.BlockSpec((B, steps, GH // 2, 128, 128), lambda c: (0, c, 0, 0, 0))
    rowspec = pl.BlockSpec((B, steps * C, GW), lambda c: (0, c, 0))
    vec = pl.BlockSpec((1, GW), lambda c: (0, 0))
    out = pl.pallas_call(
        functools.partial(_rwkv_state_kernel, batch=B, steps=steps),
        grid=(nc // steps,),
        in_specs=[matspec, matspec, rowspec, rowspec, rowspec, rowspec, vec, vec,
                  pl.BlockSpec((GW, GW), lambda c: (0, 0))],
        out_specs=rowspec,
        out_shape=jax.ShapeDtypeStruct((B, T, GW), BF16),
        scratch_shapes=[pltpu.VMEM((B, GH // 2, 128, 128), F32)],
        compiler_params=_cparams(("arbitrary",)),
        name="rwkv_state_scan",
    )(mm, nn, r3(rr), r3(y0), r3(bonus), r3(g), lnx_g, lnx_b, e512)
    return out.reshape(B * T, GW)


def _pad_cols(a, n):
    return jnp.pad(a, [(0, 0)] * (a.ndim - 1) + [(0, n - a.shape[-1])])


def _permute_in_cols(w):
    rw, sw, sg, fx = 0, 1696, 2464, 3488
    parts = [
        w[..., fx:fx + 1536],
        w[..., fx + 1544:fx + 2056],
        w[..., sg:sg + 1024],
        w[..., rw:rw + 1536],
        w[..., sw:sw + 768],
        _pad_cols(jnp.concatenate([w[..., rw + 1536:rw + 1696],
                                   w[..., fx + 1536:fx + 1544]], axis=-1), 256),
    ]
    return jnp.concatenate(parts, axis=-1)


def _tile_heads(g):
    return jnp.tile(g, GH).reshape(1, GW)


def kernel(x, c, w_mod, b_mod, norm1_g, norm2_g, w_in, w_out, rwkv_mu, rwkv_w0, rwkv_w2, rwkv_a0, rwkv_a2, rwkv_g2, rwkv_kk, rwkv_ka, rwkv_rk, rwkv_lnx_g, rwkv_lnx_b, swa_qn_g, swa_kn_g, swa_sinks, sgu_ln_g, sgu_ln_b, sgu_ws, sgu_b, fox_qn_g, fox_kn_g, fox_fb, ffn_w1, ffn_w3, ffn_w2):
    B, T, D = x.shape
    L = w_mod.shape[0]
    M = B * T

    mod = _mod_call(c, w_mod, b_mod)
    mod = mod.reshape(L, B, 6, 1, D)

    seg = jnp.arange(GW) // HEAD_DIM
    e512 = jnp.where(seg[:, None] == seg[None, :], 1.0 / HEAD_DIM, 0.0).astype(BF16)
    rows, cols = np.tril_indices(BLOCK)

    row = lambda a: a.reshape(L, 1, -1)
    tile_heads = lambda g, n: jnp.tile(g, (1, n)).reshape(L, 1, n * HEAD_DIM)
    mu_rkv = row(rwkv_mu[:, 0:3 * GW])
    mu_lr = row(_pad_cols(rwkv_mu[:, 3 * GW:], 256))
    wlr = jnp.concatenate([
        jnp.pad(rwkv_w2, ((0, 0), (0, 0), (0, 2 * GW))),
        jnp.pad(rwkv_a2, ((0, 0), (0, 0), (GW, GW))),
        jnp.pad(rwkv_g2, ((0, 0), (0, 0), (2 * GW, 0))),
        jnp.zeros((L, 256 - 160, 3 * GW), F32)], axis=1).astype(BF16)
    bias_full = jnp.repeat(jnp.swapaxes(sgu_b, 1, 2), HEAD_DIM, axis=2)
    swa_qg, swa_kg = tile_heads(swa_qn_g, GH), tile_heads(swa_kn_g, 2)
    fox_qg, fox_kg = tile_heads(fox_qn_g, GH), tile_heads(fox_kn_g, GH)
    fox_b = row(jnp.pad(fox_fb, ((0, 0), (OFF_FF % 128, 128 - GH - OFF_FF % 128))))

    x2 = x.reshape(M, D)
    for l in range(L):
        shift1, scale1, gate1, shift2, scale2, gate2 = [mod[l, :, s] for s in range(6)]
        w_in_l = _permute_in_cols(w_in[l]).astype(BF16)
        proj = _in_call(x2, norm1_g[l].reshape(1, D), scale1, shift1, w_in_l, T)

        vec = lambda a: a.reshape(1, GW)
        mm, nn, rr, y0, bonus, g = _rwkv_local_call(
            proj, mu_rkv[l], mu_lr[l], wlr[l], vec(rwkv_w0[l]), vec(rwkv_a0[l]), vec(rwkv_kk[l]),
            vec(rwkv_ka[l]), vec(rwkv_rk[l]), e512, T)
        ya = _rwkv_state_call(mm, nn, rr, y0, bonus, g, vec(rwkv_lnx_g[l]), vec(rwkv_lnx_b[l]), e512, B, T)

        yb = _swa_call(proj, swa_sinks[l], swa_qg[l], swa_kg[l], e512, T)

        w_s = jnp.zeros((GH, BLOCK, BLOCK), F32).at[:, rows, cols].set(sgu_ws[l])
        ws_cat = jnp.concatenate([w_s[0::2], w_s[1::2]], axis=2).astype(BF16)
        yc = _sgu_call(proj, vec(sgu_ln_g[l]), vec(sgu_ln_b[l]), ws_cat, bias_full[l])

        qt, ka, vt = _fox_prep_call(proj, fox_qg[l], fox_kg[l], fox_b[l], e512, T)
        yd = _fox_call(proj, qt, ka, vt, T)

        x2 = _out_call(ya, yb, yc, yd, w_out[l].astype(BF16), x2, gate1, T)
        x2 = _ffn_call(x2, norm2_g[l].reshape(1, D), scale2, shift2, gate2,
                       ffn_w1[l].astype(BF16), ffn_w3[l].astype(BF16), ffn_w2[l].astype(BF16), T)
    return x2.reshape(B, T, D)
```
